```python
import jax, jax.numpy as jnp
from jax import lax
import numpy as np

D_MODEL = 2048
BATCH = 16
SEQ = 256
DEPTH = 4
DEC_BATCH = 4
DEC_SEQ = 1024
PAST_LEN = 256

GRID_W = 64
N_MIXERS = 3
HEAD_DIM = 128
N_HEADS = D_MODEL // HEAD_DIM
N_KV_HEADS = N_HEADS // 4
KV_GROUP = N_HEADS // N_KV_HEADS
Q_BLOCK = 128
ROPE_THETA = 10000.0
AXIS_PAIRS = HEAD_DIM // 4
SGU_CHUNK = 128
SGU_WIDTH = D_MODEL
SGU_GROUPS = SGU_WIDTH // 128
SGU_GROUP_DIM = SGU_WIDTH // SGU_GROUPS
CONV_WIDTH = 3
PEER_HEADS = 8
PEER_NKEYS = 128
PEER_EXPERTS = PEER_NKEYS * PEER_NKEYS
PEER_TOPK = 16
PEER_KEY_DIM = 256
PEER_HALF = PEER_KEY_DIM // 2
TOKEN_BLOCK = 128
N_ATTN = (DEPTH + 2) // 3
N_SGU = (DEPTH + 1) // 3
N_CONV = DEPTH // 3
N_MOD = 6
EPS = 1e-6

kernel_name = "hybrid_flow_peer_step"


def rmsnorm(x, g):
    xf = x.astype(jnp.float32)
    y = xf * lax.rsqrt(jnp.mean(xf * xf, axis=-1, keepdims=True) + EPS)
    return (y * g.astype(jnp.float32)).astype(x.dtype)


def axial_rope_tables(seq_len):
    rows = seq_len // GRID_W
    t = jnp.arange(seq_len)
    r = jnp.repeat(jnp.arange(rows), GRID_W).astype(jnp.float32)
    col = (t % GRID_W).astype(jnp.float32)
    inv = ROPE_THETA ** (-jnp.arange(AXIS_PAIRS, dtype=jnp.float32) / AXIS_PAIRS)
    ang = jnp.concatenate([r[:, None] * inv, col[:, None] * inv], axis=-1)
    return jnp.cos(ang), jnp.sin(ang)


def apply_rope(x, cos, sin):
    b, s, h, d = x.shape
    xf = x.astype(jnp.float32).reshape(b, s, h, d // 2, 2)
    x0, x1 = xf[..., 0], xf[..., 1]
    cs = cos[None, :, None, :]
    sn = sin[None, :, None, :]
    out = jnp.stack([x0 * cs - x1 * sn, x0 * sn + x1 * cs], axis=-1)
    return out.reshape(x.shape).astype(x.dtype)


def qkv_heads(h, w_qkv, q_g, k_g):
    b, s, _ = h.shape
    nq = N_HEADS * HEAD_DIM
    nk = N_KV_HEADS * HEAD_DIM
    qkv = h @ w_qkv
    q = rmsnorm(qkv[..., :nq].reshape(b, s, N_HEADS, HEAD_DIM), q_g)
    k = rmsnorm(qkv[..., nq:nq + nk].reshape(b, s, N_KV_HEADS, HEAD_DIM), k_g)
    v = qkv[..., nq + nk:].reshape(b, s, N_KV_HEADS, HEAD_DIM)
    return q, k, v


def attend(q, k, v):
    b, s, _, _ = q.shape
    nb = s // Q_BLOCK
    qb = q.reshape(b, nb, Q_BLOCK, N_KV_HEADS, KV_GROUP, HEAD_DIM).transpose(1, 0, 2, 3, 4, 5)
    scale = HEAD_DIM ** -0.5

    def block(qi):
        sc = jnp.einsum('bqkgd,blkd->bkgql', qi, k).astype(jnp.float32) * scale
        p = jax.nn.softmax(sc, axis=-1).astype(v.dtype)
        return jnp.einsum('bkgql,blkd->bqkgd', p, v)

    o = lax.map(block, qb)
    return o.transpose(1, 0, 2, 3, 4, 5).reshape(b, s, N_HEADS * HEAD_DIM)


def sgu_mixer(h, w_in, g_v, w_s, b_s, w_out):
    b, s, _ = h.shape
    z = jax.nn.gelu(h @ w_in)
    u, v = z[..., :SGU_WIDTH], z[..., SGU_WIDTH:]
    v = rmsnorm(v, g_v).reshape(b, s // SGU_CHUNK, SGU_CHUNK, SGU_GROUPS, SGU_GROUP_DIM)
    mixed = jnp.einsum('gpq,bnqgc->bnpgc', w_s, v) + b_s.T[None, None, :, :, None]
    return (u * mixed.reshape(b, s, SGU_WIDTH)) @ w_out


def short_conv_mixer(h, w_in, conv_w, w_out):
    bgate, cgate, z = jnp.split(h @ w_in, 3, axis=-1)
    z = cgate * z
    s = z.shape[1]
    zp = jnp.pad(z, ((0, 0), (1, 1), (0, 0)))
    zc = conv_w[0] * zp[:, 0:s] + conv_w[1] * zp[:, 1:s + 1] + conv_w[2] * zp[:, 2:s + 2]
    return (bgate * zc) @ w_out


def peer(h, w_q, sub_keys, u_tab, v_tab):
    b, s, d = h.shape
    t = b * s
    x = h.reshape(t, d)
    q = (x @ w_q).reshape(t, PEER_HEADS, 2, PEER_HALF)
    sc = jnp.einsum('thcd,hcnd->thcn', q, sub_keys).astype(jnp.float32)
    top_s, top_i = lax.top_k(sc, PEER_TOPK)
    cand_s = (top_s[:, :, 0, :, None] + top_s[:, :, 1, None, :]).reshape(t, PEER_HEADS, PEER_TOPK * PEER_TOPK)
    cand_i = (top_i[:, :, 0, :, None] * PEER_NKEYS + top_i[:, :, 1, None, :]).reshape(t, PEER_HEADS, PEER_TOPK * PEER_TOPK)
    best_s, best_pos = lax.top_k(cand_s, PEER_TOPK)
    idx = jnp.take_along_axis(cand_i, best_pos, axis=-1)
    gate = jax.nn.softmax(best_s, axis=-1).astype(h.dtype)
    nb = t // TOKEN_BLOCK
    n_sel = PEER_HEADS * PEER_TOPK
    xs = x.reshape(nb, TOKEN_BLOCK, d)
    ids = idx.reshape(nb, TOKEN_BLOCK, n_sel)
    gs = gate.reshape(nb, TOKEN_BLOCK, n_sel)

    def block(args):
        xb, ib, gb = args
        act = jax.nn.gelu(jnp.einsum('tkd,td->tk', u_tab[ib], xb))
        return jnp.einsum('tk,tkd->td', gb * act, v_tab[ib])

    y = lax.map(block, (xs, ids, gs))
    return y.reshape(b, s, d)


def setup_inputs(seed: int = 0) -> dict:
    key = jax.random.key(seed)
    ks = jax.random.split(key, 28)
    D = D_MODEL
    n_qkv = (N_HEADS + 2 * N_KV_HEADS) * HEAD_DIM

    def nrm(k, shape, s):
        return jax.random.normal(k, shape, jnp.float32) * s

    return {
        "x_prompt": nrm(ks[0], (BATCH, SEQ, D), 1.0),
        "x_sample": nrm(ks[1], (DEC_BATCH, DEC_SEQ, D), 1.0),
        "cache_k": nrm(ks[2], (DEC_BATCH, N_ATTN, PAST_LEN, N_KV_HEADS, HEAD_DIM), 1.0),
        "cache_v": nrm(ks[3], (DEC_BATCH, N_ATTN, PAST_LEN, N_KV_HEADS, HEAD_DIM), 1.0),
        "c": nrm(ks[4], (DEC_BATCH, D), 1.0),
        "c_ctx": nrm(ks[5], (D,), 1.0),
        "norm_g": 1.0 + nrm(ks[6], (DEPTH, 2, D), 0.1),
        "final_g": 1.0 + nrm(ks[7], (D,), 0.1),
        "w_mod": nrm(ks[8], (DEPTH, D, N_MOD * D), 0.5 * D ** -0.5),
        "b_mod": nrm(ks[9], (DEPTH, N_MOD * D), 0.02),
        "attn_w_qkv": nrm(ks[10], (N_ATTN, D, n_qkv), D ** -0.5),
        "attn_q_g": 1.0 + nrm(ks[11], (N_ATTN, HEAD_DIM), 0.1),
        "attn_k_g": 1.0 + nrm(ks[12], (N_ATTN, HEAD_DIM), 0.1),
        "attn_w_o": nrm(ks[13], (N_ATTN, N_HEADS * HEAD_DIM, D), (N_HEADS * HEAD_DIM) ** -0.5),
        "sgu_w_in": nrm(ks[14], (N_SGU, D, 2 * SGU_WIDTH), D ** -0.5),
        "sgu_g": 1.0 + nrm(ks[15], (N_SGU, SGU_WIDTH), 0.1),
        "sgu_w_s": nrm(ks[16], (N_SGU, SGU_GROUPS, SGU_CHUNK, SGU_CHUNK), SGU_CHUNK ** -0.5),
        "sgu_b": nrm(ks[17], (N_SGU, SGU_GROUPS, SGU_CHUNK), 0.02),
        "sgu_w_out": nrm(ks[18], (N_SGU, SGU_WIDTH, D), SGU_WIDTH ** -0.5),
        "conv_w_in": nrm(ks[19], (N_CONV, D, 3 * D), D ** -0.5),
        "conv_w": nrm(ks[20], (N_CONV, CONV_WIDTH, D), CONV_WIDTH ** -0.5),
        "conv_w_out": nrm(ks[21], (N_CONV, D, D), D ** -0.5),
        "peer_w_q": nrm(ks[22], (DEPTH, D, PEER_HEADS * PEER_KEY_DIM), D ** -0.5),
        "peer_keys": nrm(ks[23], (DEPTH, PEER_HEADS, 2, PEER_NKEYS, PEER_HALF), PEER_HALF ** -0.5),
        "peer_u": nrm(ks[24], (DEPTH, PEER_EXPERTS, D), D ** -0.5),
        "peer_v": nrm(ks[25], (DEPTH, PEER_EXPERTS, D), PEER_HEADS ** -0.5),
    }


def reference(x_prompt, x_sample, cache_k, cache_v, c, c_ctx, norm_g, final_g, w_mod, b_mod,
              attn_w_qkv, attn_q_g, attn_k_g, attn_w_o, sgu_w_in, sgu_g, sgu_w_s, sgu_b, sgu_w_out,
              conv_w_in, conv_w, conv_w_out, peer_w_q, peer_keys, peer_u, peer_v):
    xp, xs = x_prompt, x_sample
    db = x_sample.shape[0]
    cos, sin = axial_rope_tables(x_sample.shape[1])
    new_k, new_v = [], []
    for i in range(DEPTH):
        kind, j = i % N_MIXERS, i // N_MIXERS
        modp = (jax.nn.silu(c_ctx) @ w_mod[i] + b_mod[i]).reshape(N_MOD, D_MODEL)
        mods = (jax.nn.silu(c) @ w_mod[i] + b_mod[i]).reshape(db, N_MOD, 1, D_MODEL)
        mp = [modp[n] for n in range(N_MOD)]
        ms = [mods[:, n] for n in range(N_MOD)]

        hp = rmsnorm(xp, norm_g[i, 0]) * (1 + mp[1]) + mp[0]
        hs = rmsnorm(xs, norm_g[i, 0]) * (1 + ms[1]) + ms[0]
        if kind == 0:
            qp, kp, vp = qkv_heads(hp, attn_w_qkv[j], attn_q_g[j], attn_k_g[j])
            new_k.append(kp)
            new_v.append(vp)
            op = attend(qp, kp, vp) @ attn_w_o[j]
            qs, ks_, vs = qkv_heads(hs, attn_w_qkv[j], attn_q_g[j], attn_k_g[j])
            qs = apply_rope(qs, cos, sin)
            ks_ = apply_rope(ks_, cos, sin)
            k_all = jnp.concatenate([ks_, cache_k[:, j]], axis=1)
            v_all = jnp.concatenate([vs, cache_v[:, j]], axis=1)
            os_ = attend(qs, k_all, v_all) @ attn_w_o[j]
        elif kind == 1:
            op = sgu_mixer(hp, sgu_w_in[j], sgu_g[j], sgu_w_s[j], sgu_b[j], sgu_w_out[j])
            os_ = sgu_mixer(hs, sgu_w_in[j], sgu_g[j], sgu_w_s[j], sgu_b[j], sgu_w_out[j])
        else:
            op = short_conv_mixer(hp, conv_w_in[j], conv_w[j], conv_w_out[j])
            os_ = short_conv_mixer(hs, conv_w_in[j], conv_w[j], conv_w_out[j])
        xp = xp + mp[2] * op
        xs = xs + ms[2] * os_

        hp = rmsnorm(xp, norm_g[i, 1]) * (1 + mp[4]) + mp[3]
        hs = rmsnorm(xs, norm_g[i, 1]) * (1 + ms[4]) + ms[3]
        xp = xp + mp[5] * peer(hp, peer_w_q[i], peer_keys[i], peer_u[i], peer_v[i])
        xs = xs + ms[5] * peer(hs, peer_w_q[i], peer_keys[i], peer_u[i], peer_v[i])

    y_prompt = rmsnorm(xp, final_g)
    y_sample = rmsnorm(xs, final_g)
    new_cache_k = jnp.stack(new_k, axis=1)
    new_cache_v = jnp.stack(new_v, axis=1)
    return (y_prompt, y_sample, new_cache_k, new_cache_v)
```

```python
import functools
import math

import jax
import jax.numpy as jnp
from jax import lax
from jax.experimental import pallas as pl
from jax.experimental.pallas import tpu as pltpu

F32 = jnp.float32
BF16 = jnp.bfloat16
I32 = jnp.int32

D_MODEL = 2048
HEAD_DIM = 128
N_HEADS = 16
N_KV_HEADS = 4
KV_GROUP = N_HEADS // N_KV_HEADS
GRID_W = 64
ROPE_THETA = 10000.0
AXIS_PAIRS = HEAD_DIM // 4
SGU_CHUNK = 128
SGU_GROUPS = 16
PEER_HEADS = 8
PEER_NKEYS = 128
PEER_TOPK = 16
PEER_HALF = 128
N_MOD = 6
EPS = 1e-6
N_MIXERS = 3
N_CTX_TOKENS = 16 * 256

LANES = 128
VMEM_LIMIT = 56 * 1024 * 1024

SEQ_ALIGN = 1024
MOD_ROWS = 8
TM = 1024
TN = 512
TR_NM = 256
TR_ROUTE = 256
TM_PEER = 512
E_STEP = 2 * PEER_NKEYS
NEG_INF = float("-inf")


def _cparams(sem):
    return pltpu.CompilerParams(dimension_semantics=sem, vmem_limit_bytes=VMEM_LIMIT)


def _mod_row(tile_idx, tile_rows):
    per = SEQ_ALIGN // tile_rows
    return jnp.maximum(tile_idx // per - (N_CTX_TOKENS // SEQ_ALIGN - 1), 0)


def _gelu(x):
    c = math.sqrt(2.0 / math.pi)
    return 0.5 * x * (1.0 + jnp.tanh(c * (x + 0.044715 * (x * x * x))))


def _mod_kernel(c_ref, w_ref, b_ref, o_ref):
    c = c_ref[...]
    s = c * (1.0 / (1.0 + jnp.exp(-c)))
    o_ref[...] = jnp.dot(s, w_ref[...], preferred_element_type=F32) + b_ref[...]


def _modulation(cvec, w_mod, b_mod):
    depth, d, n = w_mod.shape
    tn = 1024
    return pl.pallas_call(
        _mod_kernel,
        out_shape=jax.ShapeDtypeStruct((depth, MOD_ROWS, n), F32),
        grid=(depth, n // tn),
        in_specs=[
            pl.BlockSpec((MOD_ROWS, d), lambda l, j: (0, 0)),
            pl.BlockSpec((None, d, tn), lambda l, j: (l, 0, j)),
            pl.BlockSpec((None, 1, tn), lambda l, j: (l, 0, j)),
        ],
        out_specs=pl.BlockSpec((None, MOD_ROWS, tn), lambda l, j: (l, 0, j)),
        compiler_params=_cparams(("parallel", "parallel")),
        name="modulation",
    )(cvec, w_mod, b_mod.reshape(depth, 1, n))


def _norm_mod_kernel(has_res, x_ref, *refs):
    if has_res:
        y_ref, gate_ref, g_ref, shift_ref, scale_ref, xo_ref, h_ref = refs
        x = x_ref[...] + gate_ref[...] * y_ref[...]
        xo_ref[...] = x
    else:
        g_ref, shift_ref, scale_ref, h_ref = refs
        x = x_ref[...]
    ms = jnp.mean(x * x, axis=-1, keepdims=True)
    y = x * lax.rsqrt(ms + EPS)
    y = y * g_ref[...]
    h_ref[...] = (y * (1.0 + scale_ref[...]) + shift_ref[...]).astype(h_ref.dtype)


def _norm_mod(x, y, mods, gate_ln, shift_ln, scale_ln, g):
    t, d = x.shape
    tr = TR_NM
    row_spec = pl.BlockSpec((tr, d), lambda i: (i, 0))

    def mod_spec(ln):
        return pl.BlockSpec((None, None, None, 1, d),
                            lambda i: (ln[0], _mod_row(i, tr), ln[1], 0, 0))

    g_spec = pl.BlockSpec((1, d), lambda i: (0, 0))
    has_res = y is not None
    if has_res:
        args = (x, y, mods, g.reshape(1, d), mods, mods)
        in_specs = [row_spec, row_spec, mod_spec(gate_ln), g_spec, mod_spec(shift_ln), mod_spec(scale_ln)]
        out_shape = (jax.ShapeDtypeStruct((t, d), F32), jax.ShapeDtypeStruct((t, d), BF16))
        out_specs = (row_spec, row_spec)
    else:
        args = (x, g.reshape(1, d), mods, mods)
        in_specs = [row_spec, g_spec, mod_spec(shift_ln), mod_spec(scale_ln)]
        out_shape = jax.ShapeDtypeStruct((t, d), BF16)
        out_specs = row_spec
    out = pl.pallas_call(
        functools.partial(_norm_mod_kernel, has_res),
        out_shape=out_shape, grid=(t // tr,), in_specs=in_specs, out_specs=out_specs,
        compiler_params=_cparams(("parallel",)), name="norm_mod",
    )(*args)
    return out if has_res else (x, out)


def _final_norm_kernel(x_ref, y_ref, gate_ref, g_ref, o_ref):
    x = x_ref[...] + gate_ref[...] * y_ref[...]
    ms = jnp.mean(x * x, axis=-1, keepdims=True)
    o_ref[...] = x * lax.rsqrt(ms + EPS) * g_ref[...]


def _final_norm(x, y, mods, gate_ln, g):
    t, d = x.shape
    tr = TR_NM
    row_spec = pl.BlockSpec((tr, d), lambda i: (i, 0))
    return pl.pallas_call(
        _final_norm_kernel,
        out_shape=jax.ShapeDtypeStruct((t, d), F32), grid=(t // tr,),
        in_specs=[row_spec, row_spec,
                  pl.BlockSpec((None, None, None, 1, d),
                               lambda i: (gate_ln[0], _mod_row(i, tr), gate_ln[1], 0, 0)),
                  pl.BlockSpec((1, d), lambda i: (0, 0))],
        out_specs=row_spec,
        compiler_params=_cparams(("parallel",)), name="final_norm",
    )(x, y, mods, g.reshape(1, d))


def _matmul_kernel(act, a_ref, w_ref, o_ref):
    acc = jnp.dot(a_ref[...], w_ref[...].astype(BF16), preferred_element_type=F32)
    if act == "gelu":
        acc = _gelu(acc)
    o_ref[...] = acc.astype(o_ref.dtype)


def _matmul(a, w, out_dtype=F32, act=None):
    m, k = a.shape
    n = w.shape[1]
    tm = min(TM, m)
    return pl.pallas_call(
        functools.partial(_matmul_kernel, act),
        out_shape=jax.ShapeDtypeStruct((m, n), out_dtype),
        grid=(m // tm, n // TN),
        in_specs=[pl.BlockSpec((tm, k), lambda i, j: (i, 0)),
                  pl.BlockSpec((k, TN), lambda i, j: (0, j))],
        out_specs=pl.BlockSpec((tm, TN), lambda i, j: (i, j)),
        compiler_params=_cparams(("parallel", "arbitrary")), name="matmul",
    )(a, w)


def _rope_tables(seq_len):
    rows = seq_len // GRID_W
    t = jnp.arange(seq_len)
    r = jnp.repeat(jnp.arange(rows), GRID_W).astype(F32)
    col = (t % GRID_W).astype(F32)
    inv = ROPE_THETA ** (-jnp.arange(AXIS_PAIRS, dtype=F32) / AXIS_PAIRS)
    ang = jnp.concatenate([r[:, None] * inv, col[:, None] * inv], axis=-1)
    cos, sin = jnp.cos(ang), jnp.sin(ang)
    cos_f = jnp.repeat(cos, 2, axis=-1)
    sin_s = jnp.stack([-sin, sin], axis=-1).reshape(seq_len, HEAD_DIM)
    return cos_f, sin_s


def _qk_post_kernel(n_ctx_tiles, qkv_ref, g_ref, cos_ref, sin_ref, q_ref, k_ref):
    is_lat = pl.program_id(0) >= n_ctx_tiles
    cos = jnp.where(is_lat, cos_ref[...], 1.0)
    sin = jnp.where(is_lat, sin_ref[...], 0.0)
    lane = lax.broadcasted_iota(I32, cos.shape, 1)
    even = (lane % 2) == 0
    for hd in range(N_HEADS + N_KV_HEADS):
        x = qkv_ref[:, hd * HEAD_DIM:(hd + 1) * HEAD_DIM]
        ms = jnp.mean(x * x, axis=-1, keepdims=True)
        y = x * lax.rsqrt(ms + EPS) * g_ref[:, hd * HEAD_DIM:(hd + 1) * HEAD_DIM]
        partner = jnp.where(even, pltpu.roll(y, HEAD_DIM - 1, 1), pltpu.roll(y, 1, 1))
        y = y * cos + partner * sin
        if hd < N_HEADS:
            q_ref[:, hd * HEAD_DIM:(hd + 1) * HEAD_DIM] = y.astype(q_ref.dtype)
        else:
            kh = hd - N_HEADS
            k_ref[:, kh * HEAD_DIM:(kh + 1) * HEAD_DIM] = y


def _qk_post(qkv, g_full, cos_f, sin_s, n_ctx_tokens):
    t = qkv.shape[0]
    tr = 256
    nq, nk = N_HEADS * HEAD_DIM, N_KV_HEADS * HEAD_DIM
    lat_tiles = cos_f.shape[0] // tr
    n_ctx_tiles = n_ctx_tokens // tr

    def pos_map(i):
        return (jnp.maximum(i - n_ctx_tiles, 0) % lat_tiles, 0)

    return pl.pallas_call(
        functools.partial(_qk_post_kernel, n_ctx_tiles),
        out_shape=(jax.ShapeDtypeStruct((t, nq), BF16), jax.ShapeDtypeStruct((t, nk), F32)),
        grid=(t // tr,),
        in_specs=[pl.BlockSpec((tr, nq + nk), lambda i: (i, 0)),
                  pl.BlockSpec((1, nq + nk), lambda i: (0, 0)),
                  pl.BlockSpec((tr, HEAD_DIM), pos_map),
                  pl.BlockSpec((tr, HEAD_DIM), pos_map)],
        out_specs=(pl.BlockSpec((tr, nq), lambda i: (i, 0)), pl.BlockSpec((tr, nk), lambda i: (i, 0))),
        compiler_params=_cparams(("parallel",)), name="qk_post",
    )(qkv, g_full, cos_f, sin_s)


def _attn_kernel(has_cache, q_ref, k_ref, v_ref, *refs):
    if has_cache:
        ck_ref, cv_ref, o_ref = refs
    else:
        (o_ref,) = refs
    scale = HEAD_DIM ** -0.5
    k = k_ref[...].astype(BF16)
    v = v_ref[...].astype(BF16)
    if has_cache:
        ck = ck_ref[...].astype(BF16)
        cv = cv_ref[...].astype(BF16)
    nt = (((1,), (1,)), ((), ()))
    for g in range(KV_GROUP):
        q = q_ref[:, g * HEAD_DIM:(g + 1) * HEAD_DIM]
        s1 = lax.dot_general(q, k, nt, preferred_element_type=F32) * scale
        m = jnp.max(s1, axis=-1, keepdims=True)
        if has_cache:
            s2 = lax.dot_general(q, ck, nt, preferred_element_type=F32) * scale
            m = jnp.maximum(m, jnp.max(s2, axis=-1, keepdims=True))
        p1 = jnp.exp(s1 - m)
        l = jnp.sum(p1, axis=-1, keepdims=True)
        if has_cache:
            p2 = jnp.exp(s2 - m)
            l = l + jnp.sum(p2, axis=-1, keepdims=True)
        inv = 1.0 / l
        o = jnp.dot((p1 * inv).astype(BF16), v, preferred_element_type=F32)
        if has_cache:
            o = o + jnp.dot((p2 * inv).astype(BF16), cv, preferred_element_type=F32)
        o_ref[:, g * HEAD_DIM:(g + 1) * HEAD_DIM] = o.astype(o_ref.dtype)


def _attention(qn, kn, qkv, row0, n_seq, seq_len, cache_k=None, cache_v=None):
    tq = 256
    qb = seq_len // tq
    r0q = row0 // tq
    r0k = row0 // seq_len
    gw = KV_GROUP * HEAD_DIM
    v_col0 = (N_HEADS + N_KV_HEADS)
    has_cache = cache_k is not None
    in_specs = [
        pl.BlockSpec((tq, gw), lambda b, h, i: (r0q + b * qb + i, h)),
        pl.BlockSpec((seq_len, HEAD_DIM), lambda b, h, i: (r0k + b, h)),
        pl.BlockSpec((seq_len, HEAD_DIM), lambda b, h, i: (r0k + b, v_col0 + h)),
    ]
    args = [qn, kn, qkv]
    if has_cache:
        past = cache_k.shape[2]
        cspec = pl.BlockSpec((None, None, past, HEAD_DIM), lambda b, h, i: (b, h, 0, 0))
        in_specs += [cspec, cspec]
        args += [cache_k, cache_v]
    return pl.pallas_call(
        functools.partial(_attn_kernel, has_cache),
        out_shape=jax.ShapeDtypeStruct((n_seq * seq_len, N_HEADS * HEAD_DIM), BF16),
        grid=(n_seq, N_KV_HEADS, qb),
        in_specs=in_specs,
        out_specs=pl.BlockSpec((tq, gw), lambda b, h, i: (b * qb + i, h)),
        compiler_params=_cparams(("parallel", "parallel", "arbitrary")), name="attention",
    )(*args)


def _sgu_mid_kernel(u_ref, v_ref, g_ref, ws_ref, b_ref, o_ref):
    v = v_ref[...]
    ms = jnp.mean(v * v, axis=-1, keepdims=True)
    vn = (v * lax.rsqrt(ms + EPS) * g_ref[...]).astype(BF16)
    for g in range(SGU_GROUPS):
        sl = slice(g * LANES, (g + 1) * LANES)
        mixed = jnp.dot(ws_ref[g].astype(BF16), vn[:, sl], preferred_element_type=F32) + b_ref[:, sl]
        o_ref[:, sl] = (u_ref[:, sl] * mixed).astype(o_ref.dtype)


def _sgu_mid(z, g_v, w_s, b_full):
    t = z.shape[0]
    w = z.shape[1] // 2
    ch = SGU_CHUNK
    return pl.pallas_call(
        _sgu_mid_kernel,
        out_shape=jax.ShapeDtypeStruct((t, w), BF16),
        grid=(t // ch,),
        in_specs=[pl.BlockSpec((ch, w), lambda i: (i, 0)),
                  pl.BlockSpec((ch, w), lambda i: (i, 1)),
                  pl.BlockSpec((1, w), lambda i: (0, 0)),
                  pl.BlockSpec((SGU_GROUPS, ch, ch), lambda i: (0, 0, 0)),
                  pl.BlockSpec((ch, w), lambda i: (0, 0))],
        out_specs=pl.BlockSpec((ch, w), lambda i: (i, 0)),
        compiler_params=_cparams(("parallel",)), name="sgu_mid",
    )(z, z, g_v.reshape(1, w), w_s, b_full)


def _conv_mid_kernel(n_ctx_tiles, ctx_len, lat_len, bg_ref, cg_ref, z_ref, w_ref, o_ref):
    z = cg_ref[...] * z_ref[...]
    rows = z.shape[0]
    seq_len = jnp.where(pl.program_id(0) < n_ctx_tiles, ctx_len, lat_len)
    pos = lax.broadcasted_iota(I32, z.shape, 0) % seq_len
    prev = jnp.where(pos == 0, 0.0, pltpu.roll(z, 1, 0))
    nxt = jnp.where(pos == seq_len - 1, 0.0, pltpu.roll(z, rows - 1, 0))
    zc = w_ref[0:1, :] * prev + w_ref[1:2, :] * z + w_ref[2:3, :] * nxt
    o_ref[...] = (bg_ref[...] * zc).astype(o_ref.dtype)


def _conv_mid(y, conv_w, n_ctx_tokens, ctx_len, lat_len):
    t = y.shape[0]
    d = y.shape[1] // 3
    tr, tc = SEQ_ALIGN, 512
    nc = d // tc
    return pl.pallas_call(
        functools.partial(_conv_mid_kernel, n_ctx_tokens // tr, ctx_len, lat_len),
        out_shape=jax.ShapeDtypeStruct((t, d), BF16),
        grid=(t // tr, nc),
        in_specs=[pl.BlockSpec((tr, tc), lambda i, j: (i, j)),
                  pl.BlockSpec((tr, tc), lambda i, j: (i, nc + j)),
                  pl.BlockSpec((tr, tc), lambda i, j: (i, 2 * nc + j)),
                  pl.BlockSpec((8, tc), lambda i, j: (0, j))],
        out_specs=pl.BlockSpec((tr, tc), lambda i, j: (i, j)),
        compiler_params=_cparams(("parallel", "parallel")), name="conv_mid",
    )(y, y, y, conv_w)


_CAND = [(i, j) for i in range(PEER_TOPK) for j in range(PEER_TOPK) if (i + 1) * (j + 1) <= PEER_TOPK]


def _top_list(s, k):
    n = s.shape[0]
    iota = lax.broadcasted_iota(I32, s.shape, 0)
    rank = jnp.full(s.shape, k, I32)
    vals = []
    for r in range(k):
        m = jnp.max(s, axis=0, keepdims=True)
        idx = jnp.min(jnp.where(s == m, iota, n), axis=0, keepdims=True)
        hit = iota == idx
        rank = jnp.where(hit, r, rank)
        s = jnp.where(hit, NEG_INF, s)
        vals.append(m)
    return vals, rank


def _route_kernel(q_ref, keys_ref, m_ref, e1_ref, bit_ref, e2_ref):
    nt = (((1,), (1,)), ((), ()))
    k = PEER_TOPK
    for h in range(PEER_HEADS):
        sc = []
        for c in range(2):
            col = (2 * h + c) * PEER_HALF
            sc.append(lax.dot_general(keys_ref[h, c].astype(BF16), q_ref[:, col:col + PEER_HALF], nt,
                                      preferred_element_type=F32))
        v1, rank1 = _top_list(sc[0], k)
        v2, rank2 = _top_list(sc[1], k)
        cand = jnp.concatenate([v1[i] + v2[j] for (i, j) in _CAND], axis=0)
        _, crank = _top_list(cand, k)
        sel = crank < k
        ev1 = [jnp.exp(v - v1[0]) for v in v1]
        ev2 = [jnp.exp(v - v2[0]) for v in v2]
        zsum = jnp.zeros_like(v1[0])
        rowmask = [jnp.zeros(v1[0].shape, I32) for _ in range(k)]
        for r, (i, j) in enumerate(_CAND):
            s_r = sel[r:r + 1, :]
            zsum = zsum + jnp.where(s_r, ev1[i] * ev2[j], 0.0)
            rowmask[i] = rowmask[i] + jnp.where(s_r, 1 << j, 0)
        mask1 = jnp.zeros(rank1.shape, I32)
        for i in range(k):
            mask1 = jnp.where(rank1 == i, rowmask[i], mask1)
        m_ref[h] = mask1
        e1_ref[h] = jnp.exp(sc[0] - v1[0])
        bit_ref[h] = jnp.where(rank2 < k, lax.shift_left(jnp.ones_like(rank2), jnp.minimum(rank2, k - 1)), 0)
        e2_ref[h] = jnp.exp(sc[1] - v2[0]) / zsum


def _route(q, keys):
    t = q.shape[0]
    tr = TR_ROUTE
    shp = (PEER_HEADS, PEER_NKEYS, t)
    spec = pl.BlockSpec((PEER_HEADS, PEER_NKEYS, tr), lambda i: (0, 0, i))
    return pl.pallas_call(
        _route_kernel,
        out_shape=(jax.ShapeDtypeStruct(shp, I32), jax.ShapeDtypeStruct(shp, F32),
                   jax.ShapeDtypeStruct(shp, I32), jax.ShapeDtypeStruct(shp, F32)),
        grid=(t // tr,),
        in_specs=[pl.BlockSpec((tr, q.shape[1]), lambda i: (i, 0)),
                  pl.BlockSpec(keys.shape, lambda i: (0, 0, 0, 0))],
        out_specs=(spec, spec, spec, spec),
        compiler_params=_cparams(("parallel",)), name="peer_route",
    )(q, keys)


def _peer_dense_kernel(h_ref, m_ref, e1_ref, bit_ref, e2_ref, u_ref, v_ref, y_ref):
    p = pl.program_id(1)
    nt = (((1,), (1,)), ((), ()))
    s = lax.dot_general(h_ref[...], u_ref[...].astype(BF16), nt, preferred_element_type=F32)
    act = _gelu(s)
    halves = []
    for r in range(E_STEP // PEER_NKEYS):
        a = p * (E_STEP // PEER_NKEYS) + r
        gt = None
        for h in range(PEER_HEADS):
            m_row = m_ref[h, pl.ds(a, 1), :]
            e1_row = e1_ref[h, pl.ds(a, 1), :]
            hit = (bit_ref[h] & m_row) != 0
            term = jnp.where(hit, e2_ref[h], 0.0) * e1_row
            gt = term if gt is None else gt + term
        halves.append(gt.T)
    gate = jnp.concatenate(halves, axis=1)
    ga = (act * gate).astype(BF16)
    y = jnp.dot(ga, v_ref[...].astype(BF16), preferred_element_type=F32)

    @pl.when(p == 0)
    def _():
        y_ref[...] = y

    @pl.when(p > 0)
    def _():
        y_ref[...] += y


def _peer_dense(h, route, u_tab, v_tab):
    t, d = h.shape
    e = u_tab.shape[0]
    tm = min(TM_PEER, t)
    rspec = pl.BlockSpec((PEER_HEADS, PEER_NKEYS, tm), lambda i, p: (0, 0, i))
    return pl.pallas_call(
        _peer_dense_kernel,
        out_shape=jax.ShapeDtypeStruct((t, d), F32),
        grid=(t // tm, e // E_STEP),
        in_specs=[pl.BlockSpec((tm, d), lambda i, p: (i, 0)), rspec, rspec, rspec, rspec,
                  pl.BlockSpec((E_STEP, d), lambda i, p: (p, 0)),
                  pl.BlockSpec((E_STEP, d), lambda i, p: (p, 0))],
        out_specs=pl.BlockSpec((tm, d), lambda i, p: (i, 0)),
        compiler_params=_cparams(("parallel", "arbitrary")), name="peer_dense",
    )(h, *route, u_tab, v_tab)


def _peer(h, w_q, keys, u_tab, v_tab):
    q = _matmul(h, w_q, out_dtype=BF16)
    return _peer_dense(h, _route(q, keys), u_tab, v_tab)


def kernel(x_prompt, x_sample, cache_k, cache_v, c, c_ctx, norm_g, final_g, w_mod, b_mod, attn_w_qkv, attn_q_g, attn_k_g, attn_w_o, sgu_w_in, sgu_g, sgu_w_s, sgu_b, sgu_w_out, conv_w_in, conv_w, conv_w_out, peer_w_q, peer_keys, peer_u, peer_v):
    nb, sl, d = x_prompt.shape
    db, dsl, _ = x_sample.shape
    depth = w_mod.shape[0]
    n_ctx = nb * sl
    assert d == D_MODEL and dsl == SEQ_ALIGN and n_ctx == N_CTX_TOKENS and SEQ_ALIGN % sl == 0
    assert 1 + db <= MOD_ROWS

    x = jnp.concatenate([x_prompt.reshape(n_ctx, d), x_sample.reshape(db * dsl, d)], axis=0)
    cvec = jnp.concatenate([c_ctx[None, :], c, jnp.zeros((MOD_ROWS - 1 - db, d), F32)], axis=0)
    mods = _modulation(cvec, w_mod, b_mod).reshape(depth, MOD_ROWS, N_MOD, 1, d)

    cos_f, sin_s = _rope_tables(dsl)
    nq, nk = N_HEADS * HEAD_DIM, N_KV_HEADS * HEAD_DIM
    new_k, new_v = [], []
    y = None
    for i in range(depth):
        kind, j = i % N_MIXERS, i // N_MIXERS
        x, h = _norm_mod(x, y, mods, (i - 1, 5), (i, 0), (i, 1), norm_g[i, 0])
        if kind == 0:
            qkv = _matmul(h, attn_w_qkv[j])
            g_full = jnp.concatenate([jnp.tile(attn_q_g[j], N_HEADS), jnp.tile(attn_k_g[j], N_KV_HEADS)])
            qn, kn = _qk_post(qkv, g_full.reshape(1, nq + nk), cos_f, sin_s, n_ctx)
            new_k.append(kn[:n_ctx].reshape(nb, sl, N_KV_HEADS, HEAD_DIM))
            new_v.append(qkv[:n_ctx, nq + nk:].reshape(nb, sl, N_KV_HEADS, HEAD_DIM))
            o_ctx = _attention(qn, kn, qkv, 0, nb, sl)
            ck = jnp.transpose(cache_k[:, j], (0, 2, 1, 3))
            cv = jnp.transpose(cache_v[:, j], (0, 2, 1, 3))
            o_lat = _attention(qn, kn, qkv, n_ctx, db, dsl, ck, cv)
            o = _matmul(jnp.concatenate([o_ctx, o_lat], axis=0), attn_w_o[j])
        elif kind == 1:
            z = _matmul(h, sgu_w_in[j], act="gelu")
            b_full = jnp.repeat(sgu_b[j].T, SGU_CHUNK, axis=1)
            o = _matmul(_sgu_mid(z, sgu_g[j], sgu_w_s[j], b_full), sgu_w_out[j])
        else:
            yc = _matmul(h, conv_w_in[j])
            cw = jnp.concatenate([conv_w[j], jnp.zeros((8 - conv_w.shape[1], d), F32)], axis=0)
            o = _matmul(_conv_mid(yc, cw, n_ctx, sl, dsl), conv_w_out[j])
        x, h = _norm_mod(x, o, mods, (i, 2), (i, 3), (i, 4), norm_g[i, 1])
        y = _peer(h, peer_w_q[i], peer_keys[i], peer_u[i], peer_v[i])

    out = _final_norm(x, y, mods, (depth - 1, 5), final_g)
    y_prompt = out[:n_ctx].reshape(nb, sl, d)
    y_sample = out[n_ctx:].reshape(db, dsl, d)
    return (y_prompt, y_sample, jnp.stack(new_k, axis=1), jnp.stack(new_v, axis=1))
```

```python
import functools
import math

import jax
import jax.numpy as jnp
from jax import lax
from jax.experimental import pallas as pl
from jax.experimental.pallas import tpu as pltpu

F32 = jnp.float32
BF16 = jnp.bfloat16
I32 = jnp.int32

D_MODEL = 2048
HEAD_DIM = 128
N_HEADS = 16
N_KV_HEADS = 4
KV_GROUP = N_HEADS // N_KV_HEADS
GRID_W = 64
ROPE_THETA = 10000.0
AXIS_PAIRS = HEAD_DIM // 4
SGU_CHUNK = 128
SGU_GROUPS = 16
PEER_HEADS = 8
PEER_NKEYS = 128
PEER_TOPK = 16
PEER_HALF = 128
N_MOD = 6
EPS = 1e-6
N_MIXERS = 3
N_CTX_TOKENS = 16 * 256

LANES = 128
VMEM_LIMIT = 56 * 1024 * 1024

SEQ_ALIGN = 1024
MOD_ROWS = 8
TM = 1024
TN = 512
TR_NM = 256
TR_ROUTE = 256
TM_PEER = 1024
A_STEP = 4
E_STEP = A_STEP * PEER_NKEYS
NEG_INF = float("-inf")


def _cparams(sem):
    return pltpu.CompilerParams(dimension_semantics=sem, vmem_limit_bytes=VMEM_LIMIT)


def _mod_row(tile_idx, tile_rows):
    per = SEQ_ALIGN // tile_rows
    return jnp.maximum(tile_idx // per - (N_CTX_TOKENS // SEQ_ALIGN - 1), 0)


def _gelu(x):
    c = math.sqrt(2.0 / math.pi)
    return 0.5 * x * (1.0 + jnp.tanh(c * (x + 0.044715 * (x * x * x))))


def _mod_kernel(c_ref, w_ref, b_ref, o_ref):
    c = c_ref[...]
    s = c * (1.0 / (1.0 + jnp.exp(-c)))
    o_ref[...] = jnp.dot(s, w_ref[...], preferred_element_type=F32) + b_ref[...]


def _modulation(cvec, w_mod, b_mod):
    depth, d, n = w_mod.shape
    tn = 1024
    return pl.pallas_call(
        _mod_kernel,
        out_shape=jax.ShapeDtypeStruct((depth, MOD_ROWS, n), F32),
        grid=(depth, n // tn),
        in_specs=[
            pl.BlockSpec((MOD_ROWS, d), lambda l, j: (0, 0)),
            pl.BlockSpec((None, d, tn), lambda l, j: (l, 0, j)),
            pl.BlockSpec((None, 1, tn), lambda l, j: (l, 0, j)),
        ],
        out_specs=pl.BlockSpec((None, MOD_ROWS, tn), lambda l, j: (l, 0, j)),
        compiler_params=_cparams(("parallel", "parallel")),
        name="modulation",
    )(cvec, w_mod, b_mod.reshape(depth, 1, n))


def _norm_mod_kernel(has_res, x_ref, *refs):
    if has_res:
        y_ref, gate_ref, g_ref, shift_ref, scale_ref, xo_ref, h_ref = refs
        x = x_ref[...] + gate_ref[...] * y_ref[...]
        xo_ref[...] = x
    else:
        g_ref, shift_ref, scale_ref, h_ref = refs
        x = x_ref[...]
    ms = jnp.mean(x * x, axis=-1, keepdims=True)
    y = x * lax.rsqrt(ms + EPS)
    y = y * g_ref[...]
    h_ref[...] = (y * (1.0 + scale_ref[...]) + shift_ref[...]).astype(h_ref.dtype)


def _norm_mod(x, y, mods, gate_ln, shift_ln, scale_ln, g):
    t, d = x.shape
    tr = TR_NM
    row_spec = pl.BlockSpec((tr, d), lambda i: (i, 0))

    def mod_spec(ln):
        return pl.BlockSpec((None, None, None, 1, d),
                            lambda i: (ln[0], _mod_row(i, tr), ln[1], 0, 0))

    g_spec = pl.BlockSpec((1, d), lambda i: (0, 0))
    has_res = y is not None
    if has_res:
        args = (x, y, mods, g.reshape(1, d), mods, mods)
        in_specs = [row_spec, row_spec, mod_spec(gate_ln), g_spec, mod_spec(shift_ln), mod_spec(scale_ln)]
        out_shape = (jax.ShapeDtypeStruct((t, d), F32), jax.ShapeDtypeStruct((t, d), BF16))
        out_specs = (row_spec, row_spec)
    else:
        args = (x, g.reshape(1, d), mods, mods)
        in_specs = [row_spec, g_spec, mod_spec(shift_ln), mod_spec(scale_ln)]
        out_shape = jax.ShapeDtypeStruct((t, d), BF16)
        out_specs = row_spec
    out = pl.pallas_call(
        functools.partial(_norm_mod_kernel, has_res),
        out_shape=out_shape, grid=(t // tr,), in_specs=in_specs, out_specs=out_specs,
        compiler_params=_cparams(("parallel",)), name="norm_mod",
    )(*args)
    return out if has_res else (x, out)


def _final_norm_kernel(x_ref, y_ref, gate_ref, g_ref, o_ref):
    x = x_ref[...] + gate_ref[...] * y_ref[...]
    ms = jnp.mean(x * x, axis=-1, keepdims=True)
    o_ref[...] = x * lax.rsqrt(ms + EPS) * g_ref[...]


def _final_norm(x, y, mods, gate_ln, g):
    t, d = x.shape
    tr = TR_NM
    row_spec = pl.BlockSpec((tr, d), lambda i: (i, 0))
    return pl.pallas_call(
        _final_norm_kernel,
        out_shape=jax.ShapeDtypeStruct((t, d), F32), grid=(t // tr,),
        in_specs=[row_spec, row_spec,
                  pl.BlockSpec((None, None, None, 1, d),
                               lambda i: (gate_ln[0], _mod_row(i, tr), gate_ln[1], 0, 0)),
                  pl.BlockSpec((1, d), lambda i: (0, 0))],
        out_specs=row_spec,
        compiler_params=_cparams(("parallel",)), name="final_norm",
    )(x, y, mods, g.reshape(1, d))


def _matmul_kernel(act, a_ref, w_ref, o_ref):
    acc = jnp.dot(a_ref[...], w_ref[...].astype(BF16), preferred_element_type=F32)
    if act == "gelu":
        acc = _gelu(acc)
    o_ref[...] = acc.astype(o_ref.dtype)


def _matmul(a, w_stack, layer, out_dtype=F32, act=None):
    m, k = a.shape
    n = w_stack.shape[2]
    tm = min(TM, m)
    return pl.pallas_call(
        functools.partial(_matmul_kernel, act),
        out_shape=jax.ShapeDtypeStruct((m, n), out_dtype),
        grid=(m // tm, n // TN),
        in_specs=[pl.BlockSpec((tm, k), lambda i, j: (i, 0)),
                  pl.BlockSpec((None, k, TN), lambda i, j: (layer, 0, j))],
        out_specs=pl.BlockSpec((tm, TN), lambda i, j: (i, j)),
        compiler_params=_cparams(("parallel", "arbitrary")), name="matmul",
    )(a, w_stack)


def _rope_tables(seq_len):
    rows = seq_len // GRID_W
    t = jnp.arange(seq_len)
    r = jnp.repeat(jnp.arange(rows), GRID_W).astype(F32)
    col = (t % GRID_W).astype(F32)
    inv = ROPE_THETA ** (-jnp.arange(AXIS_PAIRS, dtype=F32) / AXIS_PAIRS)
    ang = jnp.concatenate([r[:, None] * inv, col[:, None] * inv], axis=-1)
    cos, sin = jnp.cos(ang), jnp.sin(ang)
    cos_f = jnp.repeat(cos, 2, axis=-1)
    sin_s = jnp.stack([-sin, sin], axis=-1).reshape(seq_len, HEAD_DIM)
    return cos_f, sin_s


def _qk_post_kernel(n_ctx_tiles, qkv_ref, g_ref, cos_ref, sin_ref, q_ref, k_ref):
    is_lat = pl.program_id(0) >= n_ctx_tiles
    cos = jnp.where(is_lat, cos_ref[...], 1.0)
    sin = jnp.where(is_lat, sin_ref[...], 0.0)
    lane = lax.broadcasted_iota(I32, cos.shape, 1)
    even = (lane % 2) == 0
    for hd in range(N_HEADS + N_KV_HEADS):
        x = qkv_ref[:, hd * HEAD_DIM:(hd + 1) * HEAD_DIM]
        ms = jnp.mean(x * x, axis=-1, keepdims=True)
        y = x * lax.rsqrt(ms + EPS) * g_ref[:, hd * HEAD_DIM:(hd + 1) * HEAD_DIM]
        partner = jnp.where(even, pltpu.roll(y, HEAD_DIM - 1, 1), pltpu.roll(y, 1, 1))
        y = y * cos + partner * sin
        if hd < N_HEADS:
            q_ref[:, hd * HEAD_DIM:(hd + 1) * HEAD_DIM] = y.astype(q_ref.dtype)
        else:
            kh = hd - N_HEADS
            k_ref[:, kh * HEAD_DIM:(kh + 1) * HEAD_DIM] = y


def _qk_post(qkv, g_full, cos_f, sin_s, n_ctx_tokens):
    t = qkv.shape[0]
    tr = 256
    nq, nk = N_HEADS * HEAD_DIM, N_KV_HEADS * HEAD_DIM
    lat_tiles = cos_f.shape[0] // tr
    n_ctx_tiles = n_ctx_tokens // tr

    def pos_map(i):
        return (jnp.maximum(i - n_ctx_tiles, 0) % lat_tiles, 0)

    return pl.pallas_call(
        functools.partial(_qk_post_kernel, n_ctx_tiles),
        out_shape=(jax.ShapeDtypeStruct((t, nq), BF16), jax.ShapeDtypeStruct((t, nk), F32)),
        grid=(t // tr,),
        in_specs=[pl.BlockSpec((tr, nq + nk), lambda i: (i, 0)),
                  pl.BlockSpec((1, nq + nk), lambda i: (0, 0)),
                  pl.BlockSpec((tr, HEAD_DIM), pos_map),
                  pl.BlockSpec((tr, HEAD_DIM), pos_map)],
        out_specs=(pl.BlockSpec((tr, nq), lambda i: (i, 0)), pl.BlockSpec((tr, nk), lambda i: (i, 0))),
        compiler_params=_cparams(("parallel",)), name="qk_post",
    )(qkv, g_full, cos_f, sin_s)


def _attn_kernel(has_cache, q_ref, k_ref, v_ref, *refs):
    if has_cache:
        ck_ref, cv_ref, o_ref = refs
    else:
        (o_ref,) = refs
    scale = HEAD_DIM ** -0.5
    k = k_ref[...].astype(BF16)
    v = v_ref[...].astype(BF16)
    if has_cache:
        ck = ck_ref[...].astype(BF16)
        cv = cv_ref[...].astype(BF16)
    nt = (((1,), (1,)), ((), ()))
    for g in range(KV_GROUP):
        q = q_ref[:, g * HEAD_DIM:(g + 1) * HEAD_DIM]
        s1 = lax.dot_general(q, k, nt, preferred_element_type=F32) * scale
        m = jnp.max(s1, axis=-1, keepdims=True)
        if has_cache:
            s2 = lax.dot_general(q, ck, nt, preferred_element_type=F32) * scale
            m = jnp.maximum(m, jnp.max(s2, axis=-1, keepdims=True))
        p1 = jnp.exp(s1 - m)
        l = jnp.sum(p1, axis=-1, keepdims=True)
        if has_cache:
            p2 = jnp.exp(s2 - m)
            l = l + jnp.sum(p2, axis=-1, keepdims=True)
        inv = 1.0 / l
        o = jnp.dot((p1 * inv).astype(BF16), v, preferred_element_type=F32)
        if has_cache:
            o = o + jnp.dot((p2 * inv).astype(BF16), cv, preferred_element_type=F32)
        o_ref[:, g * HEAD_DIM:(g + 1) * HEAD_DIM] = o.astype(o_ref.dtype)


def _attention(qn, kn, qkv, row0, n_seq, seq_len, cache_k=None, cache_v=None):
    tq = 256
    qb = seq_len // tq
    r0q = row0 // tq
    r0k = row0 // seq_len
    gw = KV_GROUP * HEAD_DIM
    v_col0 = (N_HEADS + N_KV_HEADS)
    has_cache = cache_k is not None
    in_specs = [
        pl.BlockSpec((tq, gw), lambda b, h, i: (r0q + b * qb + i, h)),
        pl.BlockSpec((seq_len, HEAD_DIM), lambda b, h, i: (r0k + b, h)),
        pl.BlockSpec((seq_len, HEAD_DIM), lambda b, h, i: (r0k + b, v_col0 + h)),
    ]
    args = [qn, kn, qkv]
    if has_cache:
        past = cache_k.shape[2]
        cspec = pl.BlockSpec((None, None, past, HEAD_DIM), lambda b, h, i: (b, h, 0, 0))
        in_specs += [cspec, cspec]
        args += [cache_k, cache_v]
    return pl.pallas_call(
        functools.partial(_attn_kernel, has_cache),
        out_shape=jax.ShapeDtypeStruct((n_seq * seq_len, N_HEADS * HEAD_DIM), BF16),
        grid=(n_seq, N_KV_HEADS, qb),
        in_specs=in_specs,
        out_specs=pl.BlockSpec((tq, gw), lambda b, h, i: (b * qb + i, h)),
        compiler_params=_cparams(("parallel", "parallel", "arbitrary")), name="attention",
    )(*args)


def _sgu_mid_kernel(u_ref, v_ref, g_ref, ws_ref, b_ref, o_ref):
    v = v_ref[...]
    ms = jnp.mean(v * v, axis=-1, keepdims=True)
    vn = (v * lax.rsqrt(ms + EPS) * g_ref[...]).astype(BF16)
    for g in range(SGU_GROUPS):
        sl = slice(g * LANES, (g + 1) * LANES)
        mixed = jnp.dot(ws_ref[g].astype(BF16), vn[:, sl], preferred_element_type=F32) + b_ref[:, sl]
        o_ref[:, sl] = (u_ref[:, sl] * mixed).astype(o_ref.dtype)


def _sgu_mid(z, g_v, w_s, b_full):
    t = z.shape[0]
    w = z.shape[1] // 2
    ch = SGU_CHUNK
    return pl.pallas_call(
        _sgu_mid_kernel,
        out_shape=jax.ShapeDtypeStruct((t, w), BF16),
        grid=(t // ch,),
        in_specs=[pl.BlockSpec((ch, w), lambda i: (i, 0)),
                  pl.BlockSpec((ch, w), lambda i: (i, 1)),
                  pl.BlockSpec((1, w), lambda i: (0, 0)),
                  pl.BlockSpec((SGU_GROUPS, ch, ch), lambda i: (0, 0, 0)),
                  pl.BlockSpec((ch, w), lambda i: (0, 0))],
        out_specs=pl.BlockSpec((ch, w), lambda i: (i, 0)),
        compiler_params=_cparams(("parallel",)), name="sgu_mid",
    )(z, z, g_v.reshape(1, w), w_s, b_full)


def _conv_mid_kernel(n_ctx_tiles, ctx_len, lat_len, bg_ref, cg_ref, z_ref, w_ref, o_ref):
    z = cg_ref[...] * z_ref[...]
    rows = z.shape[0]
    seq_len = jnp.where(pl.program_id(0) < n_ctx_tiles, ctx_len, lat_len)
    pos = lax.broadcasted_iota(I32, z.shape, 0) % seq_len
    prev = jnp.where(pos == 0, 0.0, pltpu.roll(z, 1, 0))
    nxt = jnp.where(pos == seq_len - 1, 0.0, pltpu.roll(z, rows - 1, 0))
    zc = w_ref[0:1, :] * prev + w_ref[1:2, :] * z + w_ref[2:3, :] * nxt
    o_ref[...] = (bg_ref[...] * zc).astype(o_ref.dtype)


def _conv_mid(y, conv_w, n_ctx_tokens, ctx_len, lat_len):
    t = y.shape[0]
    d = y.shape[1] // 3
    tr, tc = SEQ_ALIGN, 512
    nc = d // tc
    return pl.pallas_call(
        functools.partial(_conv_mid_kernel, n_ctx_tokens // tr, ctx_len, lat_len),
        out_shape=jax.ShapeDtypeStruct((t, d), BF16),
        grid=(t // tr, nc),
        in_specs=[pl.BlockSpec((tr, tc), lambda i, j: (i, j)),
                  pl.BlockSpec((tr, tc), lambda i, j: (i, nc + j)),
                  pl.BlockSpec((tr, tc), lambda i, j: (i, 2 * nc + j)),
                  pl.BlockSpec((8, tc), lambda i, j: (0, j))],
        out_specs=pl.BlockSpec((tr, tc), lambda i, j: (i, j)),
        compiler_params=_cparams(("parallel", "parallel")), name="conv_mid",
    )(y, y, y, conv_w)


_CAND = [(i, j) for i in range(PEER_TOPK) for j in range(PEER_TOPK) if (i + 1) * (j + 1) <= PEER_TOPK]


def _top_list(s, k):
    n = s.shape[0]
    iota = lax.broadcasted_iota(I32, s.shape, 0)
    rank = jnp.full(s.shape, k, I32)
    vals = []
    for r in range(k):
        m = jnp.max(s, axis=0, keepdims=True)
        idx = jnp.min(jnp.where(s == m, iota, n), axis=0, keepdims=True)
        hit = iota == idx
        rank = jnp.where(hit, r, rank)
        s = jnp.where(hit, NEG_INF, s)
        vals.append(m)
    return vals, rank


def _route_kernel(q_ref, keys_ref, len_ref, e1_ref, r2_ref, e2_ref):
    nt = (((1,), (1,)), ((), ()))
    k = PEER_TOPK
    for h in range(PEER_HEADS):
        sc = []
        for c in range(2):
            col = (2 * h + c) * PEER_HALF
            sc.append(lax.dot_general(keys_ref[h, c].astype(BF16), q_ref[:, col:col + PEER_HALF], nt,
                                      preferred_element_type=F32))
        v1, rank1 = _top_list(sc[0], k)
        v2, rank2 = _top_list(sc[1], k)
        cand = jnp.concatenate([v1[i] + v2[j] for (i, j) in _CAND], axis=0)
        _, crank = _top_list(cand, k)
        sel = crank < k
        ev1 = [jnp.exp(v - v1[0]) for v in v1]
        ev2 = [jnp.exp(v - v2[0]) for v in v2]
        zsum = jnp.zeros_like(v1[0])
        row_len = [jnp.zeros(v1[0].shape, I32) for _ in range(k)]
        for r, (i, j) in enumerate(_CAND):
            s_r = sel[r:r + 1, :]
            zsum = zsum + jnp.where(s_r, ev1[i] * ev2[j], 0.0)
            row_len[i] = row_len[i] + jnp.where(s_r, 1, 0)
        len_a = jnp.zeros(rank1.shape, I32)
        for i in range(k):
            len_a = jnp.where(rank1 == i, row_len[i], len_a)
        len_ref[h] = len_a
        e1_ref[h] = jnp.exp(sc[0] - v1[0])
        r2_ref[h] = rank2
        e2_ref[h] = jnp.exp(sc[1] - v2[0]) / zsum


def _route(q, keys_stack, layer):
    t = q.shape[0]
    tr = TR_ROUTE
    shp = (PEER_HEADS, PEER_NKEYS, t)
    spec = pl.BlockSpec((PEER_HEADS, PEER_NKEYS, tr), lambda i: (0, 0, i))
    return pl.pallas_call(
        _route_kernel,
        out_shape=(jax.ShapeDtypeStruct(shp, I32), jax.ShapeDtypeStruct(shp, F32),
                   jax.ShapeDtypeStruct(shp, I32), jax.ShapeDtypeStruct(shp, F32)),
        grid=(t // tr,),
        in_specs=[pl.BlockSpec((tr, q.shape[1]), lambda i: (i, 0)),
                  pl.BlockSpec((None,) + keys_stack.shape[1:], lambda i: (layer, 0, 0, 0, 0))],
        out_specs=(spec, spec, spec, spec),
        compiler_params=_cparams(("parallel",)), name="peer_route",
    )(q, keys_stack)


def _peer_dense_kernel(h_ref, len_ref, e1_ref, r2_ref, e2_ref, u_ref, v_ref, y_ref):
    p = pl.program_id(1)
    nt = (((1,), (1,)), ((), ()))
    s = lax.dot_general(h_ref[...], u_ref[...].astype(BF16), nt, preferred_element_type=F32)
    gates = []
    for r in range(A_STEP):
        gt = None
        for h in range(PEER_HEADS):
            hit = r2_ref[h] < len_ref[r, h:h + 1, :]
            term = jnp.where(hit, e2_ref[h], 0.0) * e1_ref[r, h:h + 1, :]
            gt = term if gt is None else gt + term
        gates.append(gt.T)
    gate = jnp.concatenate(gates, axis=1)
    ga = (_gelu(s) * gate).astype(BF16)
    y = jnp.dot(ga, v_ref[...].astype(BF16), preferred_element_type=F32)

    @pl.when(p == 0)
    def _():
        y_ref[...] = y

    @pl.when(p > 0)
    def _():
        y_ref[...] += y


def _peer_dense(h, route, u_stack, v_stack, layer):
    t, d = h.shape
    e = u_stack.shape[1]
    tm = min(TM_PEER, t)
    len_a, e1, r2, e2 = route
    len_a = jnp.transpose(len_a, (1, 0, 2))
    e1 = jnp.transpose(e1, (1, 0, 2))
    once = pl.Buffered(1)
    aspec = pl.BlockSpec((A_STEP, PEER_HEADS, tm), lambda i, p: (p, 0, i))
    bspec = pl.BlockSpec((PEER_HEADS, PEER_NKEYS, tm), lambda i, p: (0, 0, i), pipeline_mode=once)
    tspec = pl.BlockSpec((None, E_STEP, d), lambda i, p: (layer, p, 0))
    return pl.pallas_call(
        _peer_dense_kernel,
        out_shape=jax.ShapeDtypeStruct((t, d), F32),
        grid=(t // tm, e // E_STEP),
        in_specs=[pl.BlockSpec((tm, d), lambda i, p: (i, 0), pipeline_mode=once),
                  aspec, aspec, bspec, bspec, tspec, tspec],
        out_specs=pl.BlockSpec((tm, d), lambda i, p: (i, 0)),
        compiler_params=_cparams(("parallel", "arbitrary")), name="peer_dense",
    )(h, len_a, e1, r2, e2, u_stack, v_stack)


def _peer(h, w_q, keys, u_stack, v_stack, layer):
    q = _matmul(h, w_q, layer, out_dtype=BF16)
    return _peer_dense(h, _route(q, keys, layer), u_stack, v_stack, layer)


def kernel(x_prompt, x_sample, cache_k, cache_v, c, c_ctx, norm_g, final_g, w_mod, b_mod, attn_w_qkv, attn_q_g, attn_k_g, attn_w_o, sgu_w_in, sgu_g, sgu_w_s, sgu_b, sgu_w_out, conv_w_in, conv_w, conv_w_out, peer_w_q, peer_keys, peer_u, peer_v):
    nb, sl, d = x_prompt.shape
    db, dsl, _ = x_sample.shape
    depth = w_mod.shape[0]
    n_ctx = nb * sl
    assert d == D_MODEL and dsl == SEQ_ALIGN and n_ctx == N_CTX_TOKENS and SEQ_ALIGN % sl == 0
    assert 1 + db <= MOD_ROWS

    x = jnp.concatenate([x_prompt.reshape(n_ctx, d), x_sample.reshape(db * dsl, d)], axis=0)
    cvec = jnp.concatenate([c_ctx[None, :], c, jnp.zeros((MOD_ROWS - 1 - db, d), F32)], axis=0)
    mods = _modulation(cvec, w_mod, b_mod).reshape(depth, MOD_ROWS, N_MOD, 1, d)

    cos_f, sin_s = _rope_tables(dsl)
    nq, nk = N_HEADS * HEAD_DIM, N_KV_HEADS * HEAD_DIM
    new_k, new_v = [], []
    y = None
    for i in range(depth):
        kind, j = i % N_MIXERS, i // N_MIXERS
        x, h = _norm_mod(x, y, mods, (i - 1, 5), (i, 0), (i, 1), norm_g[i, 0])
        if kind == 0:
            qkv = _matmul(h, attn_w_qkv, j)
            g_full = jnp.concatenate([jnp.tile(attn_q_g[j], N_HEADS), jnp.tile(attn_k_g[j], N_KV_HEADS)])
            qn, kn = _qk_post(qkv, g_full.reshape(1, nq + nk), cos_f, sin_s, n_ctx)
            new_k.append(kn[:n_ctx].reshape(nb, sl, N_KV_HEADS, HEAD_DIM))
            new_v.append(qkv[:n_ctx, nq + nk:].reshape(nb, sl, N_KV_HEADS, HEAD_DIM))
            o_ctx = _attention(qn, kn, qkv, 0, nb, sl)
            ck = jnp.transpose(cache_k[:, j], (0, 2, 1, 3))
            cv = jnp.transpose(cache_v[:, j], (0, 2, 1, 3))
            o_lat = _attention(qn, kn, qkv, n_ctx, db, dsl, ck, cv)
            o = _matmul(jnp.concatenate([o_ctx, o_lat], axis=0), attn_w_o, j)
        elif kind == 1:
            z = _matmul(h, sgu_w_in, j, act="gelu")
            b_full = jnp.repeat(sgu_b[j].T, SGU_CHUNK, axis=1)
            o = _matmul(_sgu_mid(z, sgu_g[j], sgu_w_s[j], b_full), sgu_w_out, j)
        else:
            yc = _matmul(h, conv_w_in, j)
            cw = jnp.concatenate([conv_w[j], jnp.zeros((8 - conv_w.shape[1], d), F32)], axis=0)
            o = _matmul(_conv_mid(yc, cw, n_ctx, sl, dsl), conv_w_out, j)
        x, h = _norm_mod(x, o, mods, (i, 2), (i, 3), (i, 4), norm_g[i, 1])
        y = _peer(h, peer_w_q, peer_keys, peer_u, peer_v, i)

    out = _final_norm(x, y, mods, (depth - 1, 5), final_g)
    y_prompt = out[:n_ctx].reshape(nb, sl, d)
    y_sample = out[n_ctx:].reshape(db, dsl, d)
    return (y_prompt, y_sample, jnp.stack(new_k, axis=1), jnp.stack(new_v, axis=1))
```

```python
import functools
import math

import jax
import jax.numpy as jnp
from jax import lax
from jax.experimental import pallas as pl
from jax.experimental.pallas import tpu as pltpu

F32 = jnp.float32
BF16 = jnp.bfloat16
I32 = jnp.int32

D_MODEL = 2048
HEAD_DIM = 128
N_HEADS = 16
N_KV_HEADS = 4
KV_GROUP = N_HEADS // N_KV_HEADS
GRID_W = 64
ROPE_THETA = 10000.0
AXIS_PAIRS = HEAD_DIM // 4
SGU_CHUNK = 128
SGU_GROUPS = 16
PEER_HEADS = 8
PEER_NKEYS = 128
PEER_TOPK = 16
PEER_HALF = 128
N_MOD = 6
EPS = 1e-6
N_MIXERS = 3
N_CTX_TOKENS = 16 * 256

LANES = 128
VMEM_LIMIT = 56 * 1024 * 1024

SEQ_ALIGN = 1024
MOD_ROWS = 8
TM = 1024
TN = 512
TR_NM = 256
TR_ROUTE = 256
TM_PEER = 1024
A_STEP = 4
E_STEP = A_STEP * PEER_NKEYS
NEG_INF = float("-inf")


def _cparams(sem):
    return pltpu.CompilerParams(dimension_semantics=sem, vmem_limit_bytes=VMEM_LIMIT)


def _mod_row(tile_idx, tile_rows):
    per = SEQ_ALIGN // tile_rows
    return jnp.maximum(tile_idx // per - (N_CTX_TOKENS // SEQ_ALIGN - 1), 0)


def _gelu(x):
    c = math.sqrt(2.0 / math.pi)
    return 0.5 * x * (1.0 + jnp.tanh(c * (x + 0.044715 * (x * x * x))))


def _mod_kernel(c_ref, w_ref, b_ref, o_ref):
    c = c_ref[...]
    s = c * (1.0 / (1.0 + jnp.exp(-c)))
    o_ref[...] = jnp.dot(s, w_ref[...], preferred_element_type=F32) + b_ref[...]


def _modulation(cvec, w_mod, b_mod):
    depth, d, n = w_mod.shape
    tn = 1024
    return pl.pallas_call(
        _mod_kernel,
        out_shape=jax.ShapeDtypeStruct((depth, MOD_ROWS, n), F32),
        grid=(depth, n // tn),
        in_specs=[
            pl.BlockSpec((MOD_ROWS, d), lambda l, j: (0, 0)),
            pl.BlockSpec((None, d, tn), lambda l, j: (l, 0, j)),
            pl.BlockSpec((None, 1, tn), lambda l, j: (l, 0, j)),
        ],
        out_specs=pl.BlockSpec((None, MOD_ROWS, tn), lambda l, j: (l, 0, j)),
        compiler_params=_cparams(("parallel", "parallel")),
        name="modulation",
    )(cvec, w_mod, b_mod.reshape(depth, 1, n))


def _norm_mod_kernel(has_res, x_ref, *refs):
    if has_res:
        y_ref, gate_ref, g_ref, shift_ref, scale_ref, xo_ref, h_ref = refs
        x = x_ref[...] + gate_ref[...] * y_ref[...]
        xo_ref[...] = x
    else:
        g_ref, shift_ref, scale_ref, h_ref = refs
        x = x_ref[...]
    ms = jnp.mean(x * x, axis=-1, keepdims=True)
    y = x * lax.rsqrt(ms + EPS)
    y = y * g_ref[...]
    h_ref[...] = (y * (1.0 + scale_ref[...]) + shift_ref[...]).astype(h_ref.dtype)


def _norm_mod(x, y, mods, gate_ln, shift_ln, scale_ln, g):
    t, d = x.shape
    tr = TR_NM
    row_spec = pl.BlockSpec((tr, d), lambda i: (i, 0))

    def mod_spec(ln):
        return pl.BlockSpec((None, None, None, 1, d),
                            lambda i: (ln[0], _mod_row(i, tr), ln[1], 0, 0))

    g_spec = pl.BlockSpec((1, d), lambda i: (0, 0))
    has_res = y is not None
    if has_res:
        args = (x, y, mods, g.reshape(1, d), mods, mods)
        in_specs = [row_spec, row_spec, mod_spec(gate_ln), g_spec, mod_spec(shift_ln), mod_spec(scale_ln)]
        out_shape = (jax.ShapeDtypeStruct((t, d), F32), jax.ShapeDtypeStruct((t, d), BF16))
        out_specs = (row_spec, row_spec)
    else:
        args = (x, g.reshape(1, d), mods, mods)
        in_specs = [row_spec, g_spec, mod_spec(shift_ln), mod_spec(scale_ln)]
        out_shape = jax.ShapeDtypeStruct((t, d), BF16)
        out_specs = row_spec
    out = pl.pallas_call(
        functools.partial(_norm_mod_kernel, has_res),
        out_shape=out_shape, grid=(t // tr,), in_specs=in_specs, out_specs=out_specs,
        compiler_params=_cparams(("parallel",)), name="norm_mod",
    )(*args)
    return out if has_res else (x, out)


def _final_norm_kernel(x_ref, y_ref, gate_ref, g_ref, o_ref):
    x = x_ref[...] + gate_ref[...] * y_ref[...]
    ms = jnp.mean(x * x, axis=-1, keepdims=True)
    o_ref[...] = x * lax.rsqrt(ms + EPS) * g_ref[...]


def _final_norm(x, y, mods, gate_ln, g):
    t, d = x.shape
    tr = TR_NM
    row_spec = pl.BlockSpec((tr, d), lambda i: (i, 0))
    return pl.pallas_call(
        _final_norm_kernel,
        out_shape=jax.ShapeDtypeStruct((t, d), F32), grid=(t // tr,),
        in_specs=[row_spec, row_spec,
                  pl.BlockSpec((None, None, None, 1, d),
                               lambda i: (gate_ln[0], _mod_row(i, tr), gate_ln[1], 0, 0)),
                  pl.BlockSpec((1, d), lambda i: (0, 0))],
        out_specs=row_spec,
        compiler_params=_cparams(("parallel",)), name="final_norm",
    )(x, y, mods, g.reshape(1, d))


def _matmul_kernel(act, a_ref, w_ref, o_ref):
    acc = jnp.dot(a_ref[...], w_ref[...].astype(BF16), preferred_element_type=F32)
    if act == "gelu":
        acc = _gelu(acc)
    o_ref[...] = acc.astype(o_ref.dtype)


def _matmul(a, w_stack, layer, out_dtype=F32, act=None):
    m, k = a.shape
    n = w_stack.shape[2]
    tm = min(TM, m)
    return pl.pallas_call(
        functools.partial(_matmul_kernel, act),
        out_shape=jax.ShapeDtypeStruct((m, n), out_dtype),
        grid=(m // tm, n // TN),
        in_specs=[pl.BlockSpec((tm, k), lambda i, j: (i, 0)),
                  pl.BlockSpec((None, k, TN), lambda i, j: (layer, 0, j))],
        out_specs=pl.BlockSpec((tm, TN), lambda i, j: (i, j)),
        compiler_params=_cparams(("parallel", "arbitrary")), name="matmul",
    )(a, w_stack)


def _rope_tables(seq_len):
    rows = seq_len // GRID_W
    t = jnp.arange(seq_len)
    r = jnp.repeat(jnp.arange(rows), GRID_W).astype(F32)
    col = (t % GRID_W).astype(F32)
    inv = ROPE_THETA ** (-jnp.arange(AXIS_PAIRS, dtype=F32) / AXIS_PAIRS)
    ang = jnp.concatenate([r[:, None] * inv, col[:, None] * inv], axis=-1)
    cos, sin = jnp.cos(ang), jnp.sin(ang)
    cos_f = jnp.repeat(cos, 2, axis=-1)
    sin_s = jnp.stack([-sin, sin], axis=-1).reshape(seq_len, HEAD_DIM)
    return cos_f, sin_s


def _qk_post_kernel(n_ctx_tiles, qkv_ref, g_ref, cos_ref, sin_ref, q_ref, k_ref):
    is_lat = pl.program_id(0) >= n_ctx_tiles
    cos = jnp.where(is_lat, cos_ref[...], 1.0)
    sin = jnp.where(is_lat, sin_ref[...], 0.0)
    lane = lax.broadcasted_iota(I32, cos.shape, 1)
    even = (lane % 2) == 0
    for hd in range(N_HEADS + N_KV_HEADS):
        x = qkv_ref[:, hd * HEAD_DIM:(hd + 1) * HEAD_DIM]
        ms = jnp.mean(x * x, axis=-1, keepdims=True)
        y = x * lax.rsqrt(ms + EPS) * g_ref[:, hd * HEAD_DIM:(hd + 1) * HEAD_DIM]
        partner = jnp.where(even, pltpu.roll(y, HEAD_DIM - 1, 1), pltpu.roll(y, 1, 1))
        y = y * cos + partner * sin
        if hd < N_HEADS:
            q_ref[:, hd * HEAD_DIM:(hd + 1) * HEAD_DIM] = y.astype(q_ref.dtype)
        else:
            kh = hd - N_HEADS
            k_ref[:, kh * HEAD_DIM:(kh + 1) * HEAD_DIM] = y


def _qk_post(qkv, g_full, cos_f, sin_s, n_ctx_tokens):
    t = qkv.shape[0]
    tr = 256
    nq, nk = N_HEADS * HEAD_DIM, N_KV_HEADS * HEAD_DIM
    lat_tiles = cos_f.shape[0] // tr
    n_ctx_tiles = n_ctx_tokens // tr

    def pos_map(i):
        return (jnp.maximum(i - n_ctx_tiles, 0) % lat_tiles, 0)

    return pl.pallas_call(
        functools.partial(_qk_post_kernel, n_ctx_tiles),
        out_shape=(jax.ShapeDtypeStruct((t, nq), BF16), jax.ShapeDtypeStruct((t, nk), F32)),
        grid=(t // tr,),
        in_specs=[pl.BlockSpec((tr, nq + nk), lambda i: (i, 0)),
                  pl.BlockSpec((1, nq + nk), lambda i: (0, 0)),
                  pl.BlockSpec((tr, HEAD_DIM), pos_map),
                  pl.BlockSpec((tr, HEAD_DIM), pos_map)],
        out_specs=(pl.BlockSpec((tr, nq), lambda i: (i, 0)), pl.BlockSpec((tr, nk), lambda i: (i, 0))),
        compiler_params=_cparams(("parallel",)), name="qk_post",
    )(qkv, g_full, cos_f, sin_s)


def _attn_kernel(has_cache, q_ref, k_ref, v_ref, *refs):
    if has_cache:
        ck_ref, cv_ref, o_ref = refs
    else:
        (o_ref,) = refs
    scale = HEAD_DIM ** -0.5
    k = k_ref[...].astype(BF16)
    v = v_ref[...].astype(BF16)
    if has_cache:
        ck = ck_ref[...].astype(BF16)
        cv = cv_ref[...].astype(BF16)
    nt = (((1,), (1,)), ((), ()))
    for g in range(KV_GROUP):
        q = q_ref[:, g * HEAD_DIM:(g + 1) * HEAD_DIM]
        s1 = lax.dot_general(q, k, nt, preferred_element_type=F32) * scale
        m = jnp.max(s1, axis=-1, keepdims=True)
        if has_cache:
            s2 = lax.dot_general(q, ck, nt, preferred_element_type=F32) * scale
            m = jnp.maximum(m, jnp.max(s2, axis=-1, keepdims=True))
        p1 = jnp.exp(s1 - m)
        l = jnp.sum(p1, axis=-1, keepdims=True)
        if has_cache:
            p2 = jnp.exp(s2 - m)
            l = l + jnp.sum(p2, axis=-1, keepdims=True)
        inv = 1.0 / l
        o = jnp.dot((p1 * inv).astype(BF16), v, preferred_element_type=F32)
        if has_cache:
            o = o + jnp.dot((p2 * inv).astype(BF16), cv, preferred_element_type=F32)
        o_ref[:, g * HEAD_DIM:(g + 1) * HEAD_DIM] = o.astype(o_ref.dtype)


def _attention(qn, kn, qkv, row0, n_seq, seq_len, cache_k=None, cache_v=None):
    tq = 256
    qb = seq_len // tq
    r0q = row0 // tq
    r0k = row0 // seq_len
    gw = KV_GROUP * HEAD_DIM
    v_col0 = (N_HEADS + N_KV_HEADS)
    has_cache = cache_k is not None
    in_specs = [
        pl.BlockSpec((tq, gw), lambda b, h, i: (r0q + b * qb + i, h)),
        pl.BlockSpec((seq_len, HEAD_DIM), lambda b, h, i: (r0k + b, h)),
        pl.BlockSpec((seq_len, HEAD_DIM), lambda b, h, i: (r0k + b, v_col0 + h)),
    ]
    args = [qn, kn, qkv]
    if has_cache:
        past = cache_k.shape[2]
        cspec = pl.BlockSpec((None, None, past, HEAD_DIM), lambda b, h, i: (b, h, 0, 0))
        in_specs += [cspec, cspec]
        args += [cache_k, cache_v]
    return pl.pallas_call(
        functools.partial(_attn_kernel, has_cache),
        out_shape=jax.ShapeDtypeStruct((n_seq * seq_len, N_HEADS * HEAD_DIM), BF16),
        grid=(n_seq, N_KV_HEADS, qb),
        in_specs=in_specs,
        out_specs=pl.BlockSpec((tq, gw), lambda b, h, i: (b * qb + i, h)),
        compiler_params=_cparams(("parallel", "parallel", "arbitrary")), name="attention",
    )(*args)


def _sgu_mid_kernel(u_ref, v_ref, g_ref, ws_ref, b_ref, o_ref):
    v = v_ref[...]
    ms = jnp.mean(v * v, axis=-1, keepdims=True)
    vn = (v * lax.rsqrt(ms + EPS) * g_ref[...]).astype(BF16)
    for g in range(SGU_GROUPS):
        sl = slice(g * LANES, (g + 1) * LANES)
        mixed = jnp.dot(ws_ref[g].astype(BF16), vn[:, sl], preferred_element_type=F32) + b_ref[:, sl]
        o_ref[:, sl] = (u_ref[:, sl] * mixed).astype(o_ref.dtype)


def _sgu_mid(z, g_v, w_s, b_full):
    t = z.shape[0]
    w = z.shape[1] // 2
    ch = SGU_CHUNK
    return pl.pallas_call(
        _sgu_mid_kernel,
        out_shape=jax.ShapeDtypeStruct((t, w), BF16),
        grid=(t // ch,),
        in_specs=[pl.BlockSpec((ch, w), lambda i: (i, 0)),
                  pl.BlockSpec((ch, w), lambda i: (i, 1)),
                  pl.BlockSpec((1, w), lambda i: (0, 0)),
                  pl.BlockSpec((SGU_GROUPS, ch, ch), lambda i: (0, 0, 0)),
                  pl.BlockSpec((ch, w), lambda i: (0, 0))],
        out_specs=pl.BlockSpec((ch, w), lambda i: (i, 0)),
        compiler_params=_cparams(("parallel",)), name="sgu_mid",
    )(z, z, g_v.reshape(1, w), w_s, b_full)


def _conv_mid_kernel(n_ctx_tiles, ctx_len, lat_len, bg_ref, cg_ref, z_ref, w_ref, o_ref):
    z = cg_ref[...] * z_ref[...]
    rows = z.shape[0]
    seq_len = jnp.where(pl.program_id(0) < n_ctx_tiles, ctx_len, lat_len)
    pos = lax.broadcasted_iota(I32, z.shape, 0) % seq_len
    prev = jnp.where(pos == 0, 0.0, pltpu.roll(z, 1, 0))
    nxt = jnp.where(pos == seq_len - 1, 0.0, pltpu.roll(z, rows - 1, 0))
    zc = w_ref[0:1, :] * prev + w_ref[1:2, :] * z + w_ref[2:3, :] * nxt
    o_ref[...] = (bg_ref[...] * zc).astype(o_ref.dtype)


def _conv_mid(y, conv_w, n_ctx_tokens, ctx_len, lat_len):
    t = y.shape[0]
    d = y.shape[1] // 3
    tr, tc = SEQ_ALIGN, 512
    nc = d // tc
    return pl.pallas_call(
        functools.partial(_conv_mid_kernel, n_ctx_tokens // tr, ctx_len, lat_len),
        out_shape=jax.ShapeDtypeStruct((t, d), BF16),
        grid=(t // tr, nc),
        in_specs=[pl.BlockSpec((tr, tc), lambda i, j: (i, j)),
                  pl.BlockSpec((tr, tc), lambda i, j: (i, nc + j)),
                  pl.BlockSpec((tr, tc), lambda i, j: (i, 2 * nc + j)),
                  pl.BlockSpec((8, tc), lambda i, j: (0, j))],
        out_specs=pl.BlockSpec((tr, tc), lambda i, j: (i, j)),
        compiler_params=_cparams(("parallel", "parallel")), name="conv_mid",
    )(y, y, y, conv_w)


_CAND = [(i, j) for i in range(PEER_TOPK) for j in range(PEER_TOPK) if (i + 1) * (j + 1) <= PEER_TOPK]


def _top_list(s, k):
    n = s.shape[0]
    iota = lax.broadcasted_iota(I32, s.shape, 0)
    rank = jnp.full(s.shape, k, I32)
    vals = []
    for r in range(k):
        m = jnp.max(s, axis=0, keepdims=True)
        idx = jnp.min(jnp.where(s == m, iota, n), axis=0, keepdims=True)
        hit = iota == idx
        rank = jnp.where(hit, r, rank)
        s = jnp.where(hit, NEG_INF, s)
        vals.append(m)
    return vals, rank


def _bf16_pair_word(x):
    hi = pltpu.bitcast(x.astype(BF16).astype(F32), I32)
    return hi | lax.shift_right_logical(hi, 16)


def _route_kernel(q_ref, keys_ref, len_ref, e1_ref, r2_ref, e2_ref):
    nt = (((1,), (1,)), ((), ()))
    k = PEER_TOPK
    for h in range(PEER_HEADS):
        sc = []
        for c in range(2):
            col = (2 * h + c) * PEER_HALF
            sc.append(lax.dot_general(keys_ref[h, c].astype(BF16), q_ref[:, col:col + PEER_HALF], nt,
                                      preferred_element_type=F32))
        v1, rank1 = _top_list(sc[0], k)
        v2, rank2 = _top_list(sc[1], k)
        cand = jnp.concatenate([v1[i] + v2[j] for (i, j) in _CAND], axis=0)
        _, crank = _top_list(cand, k)
        sel = crank < k
        ev1 = [jnp.exp(v - v1[0]) for v in v1]
        ev2 = [jnp.exp(v - v2[0]) for v in v2]
        zsum = jnp.zeros_like(v1[0])
        row_len = [jnp.zeros(v1[0].shape, I32) for _ in range(k)]
        for r, (i, j) in enumerate(_CAND):
            s_r = sel[r:r + 1, :]
            zsum = zsum + jnp.where(s_r, ev1[i] * ev2[j], 0.0)
            row_len[i] = row_len[i] + jnp.where(s_r, 1, 0)
        len_a = jnp.zeros(rank1.shape, I32)
        for i in range(k):
            len_a = jnp.where(rank1 == i, row_len[i], len_a)
        len_ref[h] = _bf16_pair_word(len_a.astype(F32))
        e1_ref[h] = _bf16_pair_word(jnp.exp(sc[0] - v1[0]))
        r2_ref[h] = pltpu.bitcast(rank2.astype(BF16), I32)
        e2_ref[h] = pltpu.bitcast((jnp.exp(sc[1] - v2[0]) / zsum).astype(BF16), I32)


def _route(q, keys_stack, layer):
    t = q.shape[0]
    tr = TR_ROUTE
    shp = (PEER_HEADS, PEER_NKEYS, t)
    spec = pl.BlockSpec((PEER_HEADS, PEER_NKEYS, tr), lambda i: (0, 0, i))
    shp2 = (PEER_HEADS, PEER_NKEYS // 2, t)
    spec2 = pl.BlockSpec((PEER_HEADS, PEER_NKEYS // 2, tr), lambda i: (0, 0, i))
    return pl.pallas_call(
        _route_kernel,
        out_shape=(jax.ShapeDtypeStruct(shp, I32), jax.ShapeDtypeStruct(shp, I32),
                   jax.ShapeDtypeStruct(shp2, I32), jax.ShapeDtypeStruct(shp2, I32)),
        grid=(t // tr,),
        in_specs=[pl.BlockSpec((tr, q.shape[1]), lambda i: (i, 0)),
                  pl.BlockSpec((None,) + keys_stack.shape[1:], lambda i: (layer, 0, 0, 0, 0))],
        out_specs=(spec, spec, spec2, spec2),
        compiler_params=_cparams(("parallel",)), name="peer_route",
    )(q, keys_stack)


def _gelu_gate(x, g):
    c = 2.0 * math.sqrt(2.0 / math.pi) * math.log2(math.e)
    e = jnp.exp2(x * (-c - (c * 0.044715) * (x * x)))
    return (x * g) / (1.0 + e)


def _bf16_rows(word_row):
    words = jnp.broadcast_to(word_row, (PEER_NKEYS // 2, word_row.shape[1]))
    return pltpu.bitcast(words, BF16)


def _peer_dense_kernel(h_ref, len_ref, e1_ref, r2_ref, e2_ref, u_ref, v_ref, y_ref, s0, s1, ga0, ga1):
    step = pl.program_id(1)
    nt = (((1,), (1,)), ((), ()))
    tm = h_ref.shape[0]
    d = v_ref.shape[1]
    half_e = E_STEP // 2

    @pl.when(step == 0)
    def _():
        y_ref[...] = jnp.zeros_like(y_ref)
        for ref in (s0, s1, ga0, ga1):
            ref[...] = jnp.zeros_like(ref)

    def body(s_new, s_old, ga_new, ga_old):
        def scores(half):
            rows = slice(half * half_e, (half + 1) * half_e)
            s_new[:, rows] = lax.dot_general(h_ref[...], u_ref[rows, :].astype(BF16), nt,
                                             preferred_element_type=F32)

        def gates(r, c):
            tok = slice(c * LANES, (c + 1) * LANES)
            gt = None
            for hd in range(PEER_HEADS):
                r2 = pltpu.bitcast(r2_ref[hd, :, tok], BF16)
                e2 = pltpu.bitcast(e2_ref[hd, :, tok], BF16)
                hit = r2 < _bf16_rows(len_ref[r, hd:hd + 1, tok])
                term = jnp.where(hit, e2, 0.0) * _bf16_rows(e1_ref[r, hd:hd + 1, tok])
                gt = term if gt is None else gt + term
            cols = slice(r * PEER_NKEYS, (r + 1) * PEER_NKEYS)
            ga_new[tok, cols] = _gelu_gate(s_old[tok, cols], gt.astype(F32).T).astype(BF16)

        def contract(piece, n_pieces):
            w = d // n_pieces
            cols = slice(piece * w, (piece + 1) * w)
            y_ref[:, cols] += jnp.dot(ga_old[...], v_ref[:, cols].astype(BF16), preferred_element_type=F32)

        chunks = [(r, c) for r in range(A_STEP) for c in range(tm // LANES)]
        mxu_ops = [lambda: scores(0), lambda: contract(0, 4), lambda: contract(1, 4),
                   lambda: scores(1), lambda: contract(2, 4), lambda: contract(3, 4)]
        per = len(chunks) // len(mxu_ops)
        for k, op in enumerate(mxu_ops):
            op()
            hi = len(chunks) if k == len(mxu_ops) - 1 else (k + 1) * per
            for r, c in chunks[k * per:hi]:
                gates(r, c)

    even = lax.rem(step, 2) == 0

    @pl.when(even)
    def _():
        body(s0, s1, ga1, ga0)

    @pl.when(jnp.logical_not(even))
    def _():
        body(s1, s0, ga0, ga1)


def _peer_dense(h, route, u_stack, v_stack, layer):
    t, d = h.shape
    e = u_stack.shape[1]
    tm = min(TM_PEER, t)
    n_steps = e // E_STEP
    len_a, e1, r2, e2 = route
    len_a = jnp.transpose(len_a, (1, 0, 2))
    e1 = jnp.transpose(e1, (1, 0, 2))
    once = pl.Buffered(1)

    def blk(s, lag):
        return jnp.clip(s - lag, 0, n_steps - 1)

    aspec = pl.BlockSpec((A_STEP, PEER_HEADS, tm), lambda i, s: (blk(s, 1), 0, i))
    bspec = pl.BlockSpec((PEER_HEADS, PEER_NKEYS // 2, tm), lambda i, s: (0, 0, i), pipeline_mode=once)
    return pl.pallas_call(
        _peer_dense_kernel,
        out_shape=jax.ShapeDtypeStruct((t, d), F32),
        grid=(t // tm, n_steps + 2),
        in_specs=[pl.BlockSpec((tm, d), lambda i, s: (i, 0), pipeline_mode=once),
                  aspec, aspec, bspec, bspec,
                  pl.BlockSpec((None, E_STEP, d), lambda i, s: (layer, blk(s, 0), 0)),
                  pl.BlockSpec((None, E_STEP, d), lambda i, s: (layer, blk(s, 2), 0))],
        out_specs=pl.BlockSpec((tm, d), lambda i, s: (i, 0)),
        scratch_shapes=[pltpu.VMEM((tm, E_STEP), F32), pltpu.VMEM((tm, E_STEP), F32),
                        pltpu.VMEM((tm, E_STEP), BF16), pltpu.VMEM((tm, E_STEP), BF16)],
        compiler_params=_cparams(("parallel", "arbitrary")), name="peer_dense",
    )(h, len_a, e1, r2, e2, u_stack, v_stack)


def _peer(h, w_q, keys, u_stack, v_stack, layer):
    q = _matmul(h, w_q, layer, out_dtype=BF16)
    return _peer_dense(h, _route(q, keys, layer), u_stack, v_stack, layer)


def kernel(x_prompt, x_sample, cache_k, cache_v, c, c_ctx, norm_g, final_g, w_mod, b_mod, attn_w_qkv, attn_q_g, attn_k_g, attn_w_o, sgu_w_in, sgu_g, sgu_w_s, sgu_b, sgu_w_out, conv_w_in, conv_w, conv_w_out, peer_w_q, peer_keys, peer_u, peer_v):
    nb, sl, d = x_prompt.shape
    db, dsl, _ = x_sample.shape
    depth = w_mod.shape[0]
    n_ctx = nb * sl
    assert d == D_MODEL and dsl == SEQ_ALIGN and n_ctx == N_CTX_TOKENS and SEQ_ALIGN % sl == 0
    assert 1 + db <= MOD_ROWS

    x = jnp.concatenate([x_prompt.reshape(n_ctx, d), x_sample.reshape(db * dsl, d)], axis=0)
    cvec = jnp.concatenate([c_ctx[None, :], c, jnp.zeros((MOD_ROWS - 1 - db, d), F32)], axis=0)
    mods = _modulation(cvec, w_mod, b_mod).reshape(depth, MOD_ROWS, N_MOD, 1, d)

    cos_f, sin_s = _rope_tables(dsl)
    nq, nk = N_HEADS * HEAD_DIM, N_KV_HEADS * HEAD_DIM
    new_k, new_v = [], []
    y = None
    for i in range(depth):
        kind, j = i % N_MIXERS, i // N_MIXERS
        x, h = _norm_mod(x, y, mods, (i - 1, 5), (i, 0), (i, 1), norm_g[i, 0])
        if kind == 0:
            qkv = _matmul(h, attn_w_qkv, j)
            g_full = jnp.concatenate([jnp.tile(attn_q_g[j], N_HEADS), jnp.tile(attn_k_g[j], N_KV_HEADS)])
            qn, kn = _qk_post(qkv, g_full.reshape(1, nq + nk), cos_f, sin_s, n_ctx)
            new_k.append(kn[:n_ctx].reshape(nb, sl, N_KV_HEADS, HEAD_DIM))
            new_v.append(qkv[:n_ctx, nq + nk:].reshape(nb, sl, N_KV_HEADS, HEAD_DIM))
            o_ctx = _attention(qn, kn, qkv, 0, nb, sl)
            ck = jnp.transpose(cache_k[:, j], (0, 2, 1, 3))
            cv = jnp.transpose(cache_v[:, j], (0, 2, 1, 3))
            o_lat = _attention(qn, kn, qkv, n_ctx, db, dsl, ck, cv)
            o = _matmul(jnp.concatenate([o_ctx, o_lat], axis=0), attn_w_o, j)
        elif kind == 1:
            z = _matmul(h, sgu_w_in, j, act="gelu")
            b_full = jnp.repeat(sgu_b[j].T, SGU_CHUNK, axis=1)
            o = _matmul(_sgu_mid(z, sgu_g[j], sgu_w_s[j], b_full), sgu_w_out, j)
        else:
            yc = _matmul(h, conv_w_in, j)
            cw = jnp.concatenate([conv_w[j], jnp.zeros((8 - conv_w.shape[1], d), F32)], axis=0)
            o = _matmul(_conv_mid(yc, cw, n_ctx, sl, dsl), conv_w_out, j)
        x, h = _norm_mod(x, o, mods, (i, 2), (i, 3), (i, 4), norm_g[i, 1])
        y = _peer(h, peer_w_q, peer_keys, peer_u, peer_v, i)

    out = _final_norm(x, y, mods, (depth - 1, 5), final_g)
    y_prompt = out[:n_ctx].reshape(nb, sl, d)
    y_sample = out[n_ctx:].reshape(db, dsl, d)
    return (y_prompt, y_sample, jnp.stack(new_k, axis=1), jnp.stack(new_v, axis=1))
```

```python
import functools
import math

import jax
import jax.numpy as jnp
from jax import lax
from jax.experimental import pallas as pl
from jax.experimental.pallas import tpu as pltpu

F32 = jnp.float32
BF16 = jnp.bfloat16
I32 = jnp.int32

D_MODEL = 2048
HEAD_DIM = 128
N_HEADS = 16
N_KV_HEADS = 4
KV_GROUP = N_HEADS // N_KV_HEADS
GRID_W = 64
ROPE_THETA = 10000.0
AXIS_PAIRS = HEAD_DIM // 4
SGU_CHUNK = 128
SGU_GROUPS = 16
PEER_HEADS = 8
PEER_NKEYS = 128
PEER_TOPK = 16
PEER_HALF = 128
N_MOD = 6
EPS = 1e-6
N_MIXERS = 3
N_CTX_TOKENS = 16 * 256

LANES = 128
VMEM_LIMIT = 56 * 1024 * 1024

SEQ_ALIGN = 1024
MOD_ROWS = 8
TM = 1024
TN = 512
TR_NM = 256
TR_ROUTE = 256
TM_PEER = 1024
A_STEP = 4
E_STEP = A_STEP * PEER_NKEYS


def _cparams(sem):
    return pltpu.CompilerParams(dimension_semantics=sem, vmem_limit_bytes=VMEM_LIMIT)


def _mod_row(tile_idx, tile_rows):
    per = SEQ_ALIGN // tile_rows
    return jnp.maximum(tile_idx // per - (N_CTX_TOKENS // SEQ_ALIGN - 1), 0)


def _gelu(x):
    c = math.sqrt(2.0 / math.pi)
    return 0.5 * x * (1.0 + jnp.tanh(c * (x + 0.044715 * (x * x * x))))


def _mod_kernel(c_ref, w_ref, b_ref, o_ref):
    c = c_ref[...]
    s = c * (1.0 / (1.0 + jnp.exp(-c)))
    o_ref[...] = jnp.dot(s, w_ref[...], preferred_element_type=F32) + b_ref[...]


def _modulation(cvec, w_mod, b_mod):
    depth, d, n = w_mod.shape
    tn = 1024
    return pl.pallas_call(
        _mod_kernel,
        out_shape=jax.ShapeDtypeStruct((depth, MOD_ROWS, n), F32),
        grid=(depth, n // tn),
        in_specs=[
            pl.BlockSpec((MOD_ROWS, d), lambda l, j: (0, 0)),
            pl.BlockSpec((None, d, tn), lambda l, j: (l, 0, j)),
            pl.BlockSpec((None, 1, tn), lambda l, j: (l, 0, j)),
        ],
        out_specs=pl.BlockSpec((None, MOD_ROWS, tn), lambda l, j: (l, 0, j)),
        compiler_params=_cparams(("parallel", "parallel")),
        name="modulation",
    )(cvec, w_mod, b_mod.reshape(depth, 1, n))


def _norm_mod_kernel(has_res, x_ref, *refs):
    if has_res:
        y_ref, gate_ref, g_ref, shift_ref, scale_ref, xo_ref, h_ref = refs
        x = x_ref[...] + gate_ref[...] * y_ref[...]
        xo_ref[...] = x
    else:
        g_ref, shift_ref, scale_ref, h_ref = refs
        x = x_ref[...]
    ms = jnp.mean(x * x, axis=-1, keepdims=True)
    y = x * lax.rsqrt(ms + EPS)
    y = y * g_ref[...]
    h_ref[...] = (y * (1.0 + scale_ref[...]) + shift_ref[...]).astype(h_ref.dtype)


def _norm_mod(x, y, mods, gate_ln, shift_ln, scale_ln, g):
    t, d = x.shape
    tr = TR_NM
    row_spec = pl.BlockSpec((tr, d), lambda i: (i, 0))

    def mod_spec(ln):
        return pl.BlockSpec((None, None, None, 1, d),
                            lambda i: (ln[0], _mod_row(i, tr), ln[1], 0, 0))

    g_spec = pl.BlockSpec((1, d), lambda i: (0, 0))
    has_res = y is not None
    if has_res:
        args = (x, y, mods, g.reshape(1, d), mods, mods)
        in_specs = [row_spec, row_spec, mod_spec(gate_ln), g_spec, mod_spec(shift_ln), mod_spec(scale_ln)]
        out_shape = (jax.ShapeDtypeStruct((t, d), F32), jax.ShapeDtypeStruct((t, d), BF16))
        out_specs = (row_spec, row_spec)
    else:
        args = (x, g.reshape(1, d), mods, mods)
        in_specs = [row_spec, g_spec, mod_spec(shift_ln), mod_spec(scale_ln)]
        out_shape = jax.ShapeDtypeStruct((t, d), BF16)
        out_specs = row_spec
    out = pl.pallas_call(
        functools.partial(_norm_mod_kernel, has_res),
        out_shape=out_shape, grid=(t // tr,), in_specs=in_specs, out_specs=out_specs,
        compiler_params=_cparams(("parallel",)), name="norm_mod",
    )(*args)
    return out if has_res else (x, out)


def _final_norm_kernel(x_ref, y_ref, gate_ref, g_ref, o_ref):
    x = x_ref[...] + gate_ref[...] * y_ref[...]
    ms = jnp.mean(x * x, axis=-1, keepdims=True)
    o_ref[...] = x * lax.rsqrt(ms + EPS) * g_ref[...]


def _final_norm(x, y, mods, gate_ln, g):
    t, d = x.shape
    tr = TR_NM
    row_spec = pl.BlockSpec((tr, d), lambda i: (i, 0))
    return pl.pallas_call(
        _final_norm_kernel,
        out_shape=jax.ShapeDtypeStruct((t, d), F32), grid=(t // tr,),
        in_specs=[row_spec, row_spec,
                  pl.BlockSpec((None, None, None, 1, d),
                               lambda i: (gate_ln[0], _mod_row(i, tr), gate_ln[1], 0, 0)),
                  pl.BlockSpec((1, d), lambda i: (0, 0))],
        out_specs=row_spec,
        compiler_params=_cparams(("parallel",)), name="final_norm",
    )(x, y, mods, g.reshape(1, d))


def _matmul_kernel(act, a_ref, w_ref, o_ref):
    acc = jnp.dot(a_ref[...], w_ref[...].astype(BF16), preferred_element_type=F32)
    if act == "gelu":
        acc = _gelu(acc)
    o_ref[...] = acc.astype(o_ref.dtype)


def _matmul(a, w_stack, layer, out_dtype=F32, act=None):
    m, k = a.shape
    n = w_stack.shape[2]
    tm = min(TM, m)
    return pl.pallas_call(
        functools.partial(_matmul_kernel, act),
        out_shape=jax.ShapeDtypeStruct((m, n), out_dtype),
        grid=(m // tm, n // TN),
        in_specs=[pl.BlockSpec((tm, k), lambda i, j: (i, 0)),
                  pl.BlockSpec((None, k, TN), lambda i, j: (layer, 0, j))],
        out_specs=pl.BlockSpec((tm, TN), lambda i, j: (i, j)),
        compiler_params=_cparams(("parallel", "arbitrary")), name="matmul",
    )(a, w_stack)


def _rope_tables(seq_len):
    rows = seq_len // GRID_W
    t = jnp.arange(seq_len)
    r = jnp.repeat(jnp.arange(rows), GRID_W).astype(F32)
    col = (t % GRID_W).astype(F32)
    inv = ROPE_THETA ** (-jnp.arange(AXIS_PAIRS, dtype=F32) / AXIS_PAIRS)
    ang = jnp.concatenate([r[:, None] * inv, col[:, None] * inv], axis=-1)
    cos, sin = jnp.cos(ang), jnp.sin(ang)
    cos_f = jnp.repeat(cos, 2, axis=-1)
    sin_s = jnp.stack([-sin, sin], axis=-1).reshape(seq_len, HEAD_DIM)
    return cos_f, sin_s


def _qk_post_kernel(n_ctx_tiles, qkv_ref, g_ref, cos_ref, sin_ref, q_ref, k_ref):
    is_lat = pl.program_id(0) >= n_ctx_tiles
    cos = jnp.where(is_lat, cos_ref[...], 1.0)
    sin = jnp.where(is_lat, sin_ref[...], 0.0)
    lane = lax.broadcasted_iota(I32, cos.shape, 1)
    even = (lane % 2) == 0
    for hd in range(N_HEADS + N_KV_HEADS):
        x = qkv_ref[:, hd * HEAD_DIM:(hd + 1) * HEAD_DIM]
        ms = jnp.mean(x * x, axis=-1, keepdims=True)
        y = x * lax.rsqrt(ms + EPS) * g_ref[:, hd * HEAD_DIM:(hd + 1) * HEAD_DIM]
        partner = jnp.where(even, pltpu.roll(y, HEAD_DIM - 1, 1), pltpu.roll(y, 1, 1))
        y = y * cos + partner * sin
        if hd < N_HEADS:
            q_ref[:, hd * HEAD_DIM:(hd + 1) * HEAD_DIM] = y.astype(q_ref.dtype)
        else:
            kh = hd - N_HEADS
            k_ref[:, kh * HEAD_DIM:(kh + 1) * HEAD_DIM] = y


def _qk_post(qkv, g_full, cos_f, sin_s, n_ctx_tokens):
    t = qkv.shape[0]
    tr = 256
    nq, nk = N_HEADS * HEAD_DIM, N_KV_HEADS * HEAD_DIM
    lat_tiles = cos_f.shape[0] // tr
    n_ctx_tiles = n_ctx_tokens // tr

    def pos_map(i):
        return (jnp.maximum(i - n_ctx_tiles, 0) % lat_tiles, 0)

    return pl.pallas_call(
        functools.partial(_qk_post_kernel, n_ctx_tiles),
        out_shape=(jax.ShapeDtypeStruct((t, nq), BF16), jax.ShapeDtypeStruct((t, nk), F32)),
        grid=(t // tr,),
        in_specs=[pl.BlockSpec((tr, nq + nk), lambda i: (i, 0)),
                  pl.BlockSpec((1, nq + nk), lambda i: (0, 0)),
                  pl.BlockSpec((tr, HEAD_DIM), pos_map),
                  pl.BlockSpec((tr, HEAD_DIM), pos_map)],
        out_specs=(pl.BlockSpec((tr, nq), lambda i: (i, 0)), pl.BlockSpec((tr, nk), lambda i: (i, 0))),
        compiler_params=_cparams(("parallel",)), name="qk_post",
    )(qkv, g_full, cos_f, sin_s)


def _attn_kernel(has_cache, q_ref, k_ref, v_ref, *refs):
    if has_cache:
        ck_ref, cv_ref, o_ref = refs
    else:
        (o_ref,) = refs
    scale = HEAD_DIM ** -0.5
    k = k_ref[...].astype(BF16)
    v = v_ref[...].astype(BF16)
    if has_cache:
        ck = ck_ref[...].astype(BF16)
        cv = cv_ref[...].astype(BF16)
    nt = (((1,), (1,)), ((), ()))
    for g in range(KV_GROUP):
        q = q_ref[:, g * HEAD_DIM:(g + 1) * HEAD_DIM]
        s1 = lax.dot_general(q, k, nt, preferred_element_type=F32) * scale
        m = jnp.max(s1, axis=-1, keepdims=True)
        if has_cache:
            s2 = lax.dot_general(q, ck, nt, preferred_element_type=F32) * scale
            m = jnp.maximum(m, jnp.max(s2, axis=-1, keepdims=True))
        p1 = jnp.exp(s1 - m)
        l = jnp.sum(p1, axis=-1, keepdims=True)
        if has_cache:
            p2 = jnp.exp(s2 - m)
            l = l + jnp.sum(p2, axis=-1, keepdims=True)
        inv = 1.0 / l
        o = jnp.dot((p1 * inv).astype(BF16), v, preferred_element_type=F32)
        if has_cache:
            o = o + jnp.dot((p2 * inv).astype(BF16), cv, preferred_element_type=F32)
        o_ref[:, g * HEAD_DIM:(g + 1) * HEAD_DIM] = o.astype(o_ref.dtype)


def _attention(qn, kn, qkv, row0, n_seq, seq_len, cache_k=None, cache_v=None):
    tq = 256
    qb = seq_len // tq
    r0q = row0 // tq
    r0k = row0 // seq_len
    gw = KV_GROUP * HEAD_DIM
    v_col0 = (N_HEADS + N_KV_HEADS)
    has_cache = cache_k is not None
    in_specs = [
        pl.BlockSpec((tq, gw), lambda b, h, i: (r0q + b * qb + i, h)),
        pl.BlockSpec((seq_len, HEAD_DIM), lambda b, h, i: (r0k + b, h)),
        pl.BlockSpec((seq_len, HEAD_DIM), lambda b, h, i: (r0k + b, v_col0 + h)),
    ]
    args = [qn, kn, qkv]
    if has_cache:
        past = cache_k.shape[2]
        cspec = pl.BlockSpec((None, None, past, HEAD_DIM), lambda b, h, i: (b, h, 0, 0))
        in_specs += [cspec, cspec]
        args += [cache_k, cache_v]
    return pl.pallas_call(
        functools.partial(_attn_kernel, has_cache),
        out_shape=jax.ShapeDtypeStruct((n_seq * seq_len, N_HEADS * HEAD_DIM), BF16),
        grid=(n_seq, N_KV_HEADS, qb),
        in_specs=in_specs,
        out_specs=pl.BlockSpec((tq, gw), lambda b, h, i: (b * qb + i, h)),
        compiler_params=_cparams(("parallel", "parallel", "arbitrary")), name="attention",
    )(*args)


def _sgu_mid_kernel(u_ref, v_ref, g_ref, ws_ref, b_ref, o_ref):
    v = v_ref[...]
    ms = jnp.mean(v * v, axis=-1, keepdims=True)
    vn = (v * lax.rsqrt(ms + EPS) * g_ref[...]).astype(BF16)
    for g in range(SGU_GROUPS):
        sl = slice(g * LANES, (g + 1) * LANES)
        mixed = jnp.dot(ws_ref[g].astype(BF16), vn[:, sl], preferred_element_type=F32) + b_ref[:, sl]
        o_ref[:, sl] = (u_ref[:, sl] * mixed).astype(o_ref.dtype)


def _sgu_mid(z, g_v, w_s, b_full):
    t = z.shape[0]
    w = z.shape[1] // 2
    ch = SGU_CHUNK
    return pl.pallas_call(
        _sgu_mid_kernel,
        out_shape=jax.ShapeDtypeStruct((t, w), BF16),
        grid=(t // ch,),
        in_specs=[pl.BlockSpec((ch, w), lambda i: (i, 0)),
                  pl.BlockSpec((ch, w), lambda i: (i, 1)),
                  pl.BlockSpec((1, w), lambda i: (0, 0)),
                  pl.BlockSpec((SGU_GROUPS, ch, ch), lambda i: (0, 0, 0)),
                  pl.BlockSpec((ch, w), lambda i: (0, 0))],
        out_specs=pl.BlockSpec((ch, w), lambda i: (i, 0)),
        compiler_params=_cparams(("parallel",)), name="sgu_mid",
    )(z, z, g_v.reshape(1, w), w_s, b_full)


def _conv_mid_kernel(n_ctx_tiles, ctx_len, lat_len, bg_ref, cg_ref, z_ref, w_ref, o_ref):
    z = cg_ref[...] * z_ref[...]
    rows = z.shape[0]
    seq_len = jnp.where(pl.program_id(0) < n_ctx_tiles, ctx_len, lat_len)
    pos = lax.broadcasted_iota(I32, z.shape, 0) % seq_len
    prev = jnp.where(pos == 0, 0.0, pltpu.roll(z, 1, 0))
    nxt = jnp.where(pos == seq_len - 1, 0.0, pltpu.roll(z, rows - 1, 0))
    zc = w_ref[0:1, :] * prev + w_ref[1:2, :] * z + w_ref[2:3, :] * nxt
    o_ref[...] = (bg_ref[...] * zc).astype(o_ref.dtype)


def _conv_mid(y, conv_w, n_ctx_tokens, ctx_len, lat_len):
    t = y.shape[0]
    d = y.shape[1] // 3
    tr, tc = SEQ_ALIGN, 512
    nc = d // tc
    return pl.pallas_call(
        functools.partial(_conv_mid_kernel, n_ctx_tokens // tr, ctx_len, lat_len),
        out_shape=jax.ShapeDtypeStruct((t, d), BF16),
        grid=(t // tr, nc),
        in_specs=[pl.BlockSpec((tr, tc), lambda i, j: (i, j)),
                  pl.BlockSpec((tr, tc), lambda i, j: (i, nc + j)),
                  pl.BlockSpec((tr, tc), lambda i, j: (i, 2 * nc + j)),
                  pl.BlockSpec((8, tc), lambda i, j: (0, j))],
        out_specs=pl.BlockSpec((tr, tc), lambda i, j: (i, j)),
        compiler_params=_cparams(("parallel", "parallel")), name="conv_mid",
    )(y, y, y, conv_w)


_CAND = [(i, j) for i in range(PEER_TOPK) for j in range(PEER_TOPK) if (i + 1) * (j + 1) <= PEER_TOPK]


def _top_list(s, k, exact_ties):
    n = s.shape[0]
    if exact_ties:
        iota = lax.broadcasted_iota(I32, s.shape, 0)
    vals = []
    for r in range(k):
        m = jnp.max(s, axis=0, keepdims=True)
        hit = s == m
        if exact_ties:
            idx = jnp.min(jnp.where(hit, iota, n), axis=0, keepdims=True)
            hit = iota == idx
        s = jnp.where(hit, -_TAKEN * (1.0 + r / k), s)
        vals.append(m)
    rank = jnp.where(s < -0.5 * _TAKEN, (s * (-1.0 / _TAKEN) - 1.0) * k, float(k))
    return vals, rank


_TAKEN = 2.0 ** 126


def _ranked_count_off(rank, k):
    return jnp.sum(jnp.where(rank < k, 1, 0), axis=0, keepdims=True) != k


def _bf16_pair_word(x):
    hi = pltpu.bitcast(x.astype(BF16).astype(F32), I32)
    return hi | lax.shift_right_logical(hi, 16)


def _route_kernel(exact_ties, q_ref, keys_ref, len_ref, e1_ref, r2_ref, e2_ref, *flag_ref):
    nt = (((1,), (1,)), ((), ()))
    k = PEER_TOPK
    for h in range(PEER_HEADS):
        sc = []
        for c in range(2):
            col = (2 * h + c) * PEER_HALF
            sc.append(lax.dot_general(keys_ref[h, c].astype(BF16), q_ref[:, col:col + PEER_HALF], nt,
                                      preferred_element_type=F32))
        v1, rank1 = _top_list(sc[0], k, exact_ties)
        v2, rank2 = _top_list(sc[1], k, exact_ties)
        cand = jnp.concatenate([v1[i] + v2[j] for (i, j) in _CAND], axis=0)
        _, crank = _top_list(cand, k, exact_ties)
        if not exact_ties:
            tie = _ranked_count_off(rank1, k) | _ranked_count_off(rank2, k) | _ranked_count_off(crank, k)
            flag_ref[0][h:h + 1, :] = jnp.where(tie, 1, 0)
        sel = crank < k
        ev1 = [jnp.exp(v - v1[0]) for v in v1]
        ev2 = [jnp.exp(v - v2[0]) for v in v2]
        zsum = jnp.zeros_like(v1[0])
        row_len = [jnp.zeros(v1[0].shape, F32) for _ in range(k)]
        for r, (i, j) in enumerate(_CAND):
            s_r = sel[r:r + 1, :]
            zsum = zsum + jnp.where(s_r, ev1[i] * ev2[j], 0.0)
            row_len[i] = row_len[i] + jnp.where(s_r, 1.0, 0.0)
        len_a = jnp.zeros(rank1.shape, F32)
        for i in range(k):
            len_a = jnp.where(rank1 == i, row_len[i], len_a)
        len_ref[h] = _bf16_pair_word(len_a)
        e1_ref[h] = _bf16_pair_word(jnp.exp(sc[0] - v1[0]))
        r2_ref[h] = pltpu.bitcast(rank2.astype(BF16), I32)
        e2_ref[h] = pltpu.bitcast((jnp.exp(sc[1] - v2[0]) / zsum).astype(BF16), I32)


def _route_call(q, keys_stack, layer, exact_ties):
    t = q.shape[0]
    tr = TR_ROUTE
    shp = (PEER_HEADS, PEER_NKEYS, t)
    spec = pl.BlockSpec((PEER_HEADS, PEER_NKEYS, tr), lambda i: (0, 0, i))
    shp2 = (PEER_HEADS, PEER_NKEYS // 2, t)
    spec2 = pl.BlockSpec((PEER_HEADS, PEER_NKEYS // 2, tr), lambda i: (0, 0, i))
    out_shape = [jax.ShapeDtypeStruct(shp, I32), jax.ShapeDtypeStruct(shp, I32),
                 jax.ShapeDtypeStruct(shp2, I32), jax.ShapeDtypeStruct(shp2, I32)]
    out_specs = [spec, spec, spec2, spec2]
    if not exact_ties:
        out_shape.append(jax.ShapeDtypeStruct((PEER_HEADS, t), I32))
        out_specs.append(pl.BlockSpec((PEER_HEADS, tr), lambda i: (0, i)))
    return pl.pallas_call(
        functools.partial(_route_kernel, exact_ties),
        out_shape=tuple(out_shape),
        grid=(t // tr,),
        in_specs=[pl.BlockSpec((tr, q.shape[1]), lambda i: (i, 0)),
                  pl.BlockSpec((None,) + keys_stack.shape[1:], lambda i: (layer, 0, 0, 0, 0))],
        out_specs=tuple(out_specs),
        compiler_params=_cparams(("parallel",)),
        name="peer_route_exact" if exact_ties else "peer_route",
    )(q, keys_stack)


def _route(q, keys_stack, layer):
    *fast, flags = _route_call(q, keys_stack, layer, False)
    return lax.cond(jnp.any(flags != 0),
                    lambda: tuple(_route_call(q, keys_stack, layer, True)),
                    lambda: tuple(fast))


def _gelu_gate(x, g):
    c = 2.0 * math.sqrt(2.0 / math.pi) * math.log2(math.e)
    e = jnp.exp2(x * (-c - (c * 0.044715) * (x * x)))
    return (x * g) / (1.0 + e)


def _bf16_rows(word_row):
    words = jnp.broadcast_to(word_row, (PEER_NKEYS // 2, word_row.shape[1]))
    return pltpu.bitcast(words, BF16)


def _peer_dense_kernel(n_steps, h_ref, len_ref, e1_ref, r2_ref, e2_ref, u_ref, v_ref, y_ref, s0, s1, ga0, ga1):
    step = pl.program_id(0)
    nt = (((1,), (1,)), ((), ()))
    tm = h_ref.shape[0]
    d = v_ref.shape[1]
    half_e = E_STEP // 2

    @pl.when(step == 0)
    def _():
        for ref in (s0, s1, ga0, ga1):
            ref[...] = jnp.zeros_like(ref)

    @pl.when((step == 0) | (lax.rem(step + n_steps - 2, n_steps) == 0))
    def _():
        y_ref[...] = jnp.zeros_like(y_ref)

    def body(s_new, s_old, ga_new, ga_old):
        def scores(half):
            rows = slice(half * half_e, (half + 1) * half_e)
            s_new[:, rows] = lax.dot_general(h_ref[...], u_ref[rows, :].astype(BF16), nt,
                                             preferred_element_type=F32)

        def gates(r, c):
            tok = slice(c * LANES, (c + 1) * LANES)
            gt = None
            for hd in range(PEER_HEADS):
                r2 = pltpu.bitcast(r2_ref[hd, :, tok], BF16)
                e2 = pltpu.bitcast(e2_ref[hd, :, tok], BF16)
                hit = r2 < _bf16_rows(len_ref[r, hd:hd + 1, tok])
                term = jnp.where(hit, e2, 0.0) * _bf16_rows(e1_ref[r, hd:hd + 1, tok])
                gt = term if gt is None else gt + term
            cols = slice(r * PEER_NKEYS, (r + 1) * PEER_NKEYS)
            ga_new[tok, cols] = _gelu_gate(s_old[tok, cols], gt.astype(F32).T).astype(BF16)

        def contract(piece, n_pieces):
            w = d // n_pieces
            cols = slice(piece * w, (piece + 1) * w)
            y_ref[:, cols] += jnp.dot(ga_old[...], v_ref[:, cols].astype(BF16), preferred_element_type=F32)

        chunks = [(r, c) for r in range(A_STEP) for c in range(tm // LANES)]
        mxu_ops = [lambda: scores(0), lambda: contract(0, 4), lambda: contract(1, 4),
                   lambda: scores(1), lambda: contract(2, 4), lambda: contract(3, 4)]
        per = len(chunks) // len(mxu_ops)
        for k, op in enumerate(mxu_ops):
            op()
            hi = len(chunks) if k == len(mxu_ops) - 1 else (k + 1) * per
            for r, c in chunks[k * per:hi]:
                gates(r, c)

    even = lax.rem(step, 2) == 0

    @pl.when(even)
    def _():
        body(s0, s1, ga1, ga0)

    @pl.when(jnp.logical_not(even))
    def _():
        body(s1, s0, ga0, ga1)


def _peer_dense(h, route, u_stack, v_stack, layer):
    t, d = h.shape
    e = u_stack.shape[1]
    tm = min(TM_PEER, t)
    n_steps = e // E_STEP
    len_a, e1, r2, e2 = route
    len_a = jnp.transpose(len_a, (1, 0, 2))
    e1 = jnp.transpose(e1, (1, 0, 2))
    once = pl.Buffered(1)
    n_pairs = (t // tm) * n_steps

    def pair(g, lag):
        p = jnp.clip(g - lag, 0, n_pairs - 1)
        return p // n_steps, p % n_steps

    aspec = pl.BlockSpec((A_STEP, PEER_HEADS, tm), lambda g: (pair(g, 1)[1], 0, pair(g, 1)[0]))
    bspec = pl.BlockSpec((PEER_HEADS, PEER_NKEYS // 2, tm), lambda g: (0, 0, pair(g, 1)[0]), pipeline_mode=once)
    return pl.pallas_call(
        functools.partial(_peer_dense_kernel, n_steps),
        out_shape=jax.ShapeDtypeStruct((t, d), F32),
        grid=(n_pairs + 2,),
        in_specs=[pl.BlockSpec((tm, d), lambda g: (pair(g, 0)[0], 0), pipeline_mode=once),
                  aspec, aspec, bspec, bspec,
                  pl.BlockSpec((None, E_STEP, d), lambda g: (layer, pair(g, 0)[1], 0)),
                  pl.BlockSpec((None, E_STEP, d), lambda g: (layer, pair(g, 2)[1], 0))],
        out_specs=pl.BlockSpec((tm, d), lambda g: (pair(g, 2)[0], 0)),
        scratch_shapes=[pltpu.VMEM((tm, E_STEP), F32), pltpu.VMEM((tm, E_STEP), F32),
                        pltpu.VMEM((tm, E_STEP), BF16), pltpu.VMEM((tm, E_STEP), BF16)],
        compiler_params=_cparams(("arbitrary",)), name="peer_dense",
    )(h, len_a, e1, r2, e2, u_stack, v_stack)


def _peer(h, w_q, keys, u_stack, v_stack, layer):
    q = _matmul(h, w_q, layer, out_dtype=BF16)
    return _peer_dense(h, _route(q, keys, layer), u_stack, v_stack, layer)


def kernel(x_prompt, x_sample, cache_k, cache_v, c, c_ctx, norm_g, final_g, w_mod, b_mod, attn_w_qkv, attn_q_g, attn_k_g, attn_w_o, sgu_w_in, sgu_g, sgu_w_s, sgu_b, sgu_w_out, conv_w_in, conv_w, conv_w_out, peer_w_q, peer_keys, peer_u, peer_v):
    nb, sl, d = x_prompt.shape
    db, dsl, _ = x_sample.shape
    depth = w_mod.shape[0]
    n_ctx = nb * sl
    assert d == D_MODEL and dsl == SEQ_ALIGN and n_ctx == N_CTX_TOKENS and SEQ_ALIGN % sl == 0
    assert 1 + db <= MOD_ROWS

    x = jnp.concatenate([x_prompt.reshape(n_ctx, d), x_sample.reshape(db * dsl, d)], axis=0)
    cvec = jnp.concatenate([c_ctx[None, :], c, jnp.zeros((MOD_ROWS - 1 - db, d), F32)], axis=0)
    mods = _modulation(cvec, w_mod, b_mod).reshape(depth, MOD_ROWS, N_MOD, 1, d)

    cos_f, sin_s = _rope_tables(dsl)
    nq, nk = N_HEADS * HEAD_DIM, N_KV_HEADS * HEAD_DIM
    new_k, new_v = [], []
    y = None
    for i in range(depth):
        kind, j = i % N_MIXERS, i // N_MIXERS
        x, h = _norm_mod(x, y, mods, (i - 1, 5), (i, 0), (i, 1), norm_g[i, 0])
        if kind == 0:
            qkv = _matmul(h, attn_w_qkv, j)
            g_full = jnp.concatenate([jnp.tile(attn_q_g[j], N_HEADS), jnp.tile(attn_k_g[j], N_KV_HEADS)])
            qn, kn = _qk_post(qkv, g_full.reshape(1, nq + nk), cos_f, sin_s, n_ctx)
            new_k.append(kn[:n_ctx].reshape(nb, sl, N_KV_HEADS, HEAD_DIM))
            new_v.append(qkv[:n_ctx, nq + nk:].reshape(nb, sl, N_KV_HEADS, HEAD_DIM))
            o_ctx = _attention(qn, kn, qkv, 0, nb, sl)
            ck = jnp.transpose(cache_k[:, j], (0, 2, 1, 3))
            cv = jnp.transpose(cache_v[:, j], (0, 2, 1, 3))
            o_lat = _attention(qn, kn, qkv, n_ctx, db, dsl, ck, cv)
            o = _matmul(jnp.concatenate([o_ctx, o_lat], axis=0), attn_w_o, j)
        elif kind == 1:
            z = _matmul(h, sgu_w_in, j, act="gelu")
            b_full = jnp.repeat(sgu_b[j].T, SGU_CHUNK, axis=1)
            o = _matmul(_sgu_mid(z, sgu_g[j], sgu_w_s[j], b_full), sgu_w_out, j)
        else:
            yc = _matmul(h, conv_w_in, j)
            cw = jnp.concatenate([conv_w[j], jnp.zeros((8 - conv_w.shape[1], d), F32)], axis=0)
            o = _matmul(_conv_mid(yc, cw, n_ctx, sl, dsl), conv_w_out, j)
        x, h = _norm_mod(x, o, mods, (i, 2), (i, 3), (i, 4), norm_g[i, 1])
        y = _peer(h, peer_w_q, peer_keys, peer_u, peer_v, i)

    out = _final_norm(x, y, mods, (depth - 1, 5), final_g)
    y_prompt = out[:n_ctx].reshape(nb, sl, d)
    y_sample = out[n_ctx:].reshape(db, dsl, d)
    return (y_prompt, y_sample, jnp.stack(new_k, axis=1), jnp.stack(new_v, axis=1))
```

```python
import functools
import math

import jax
import jax.numpy as jnp
from jax import lax
from jax.experimental import pallas as pl
from jax.experimental.pallas import tpu as pltpu

F32 = jnp.float32
BF16 = jnp.bfloat16
I32 = jnp.int32

D_MODEL = 2048
HEAD_DIM = 128
N_HEADS = 16
N_KV_HEADS = 4
KV_GROUP = N_HEADS // N_KV_HEADS
GRID_W = 64
ROPE_THETA = 10000.0
AXIS_PAIRS = HEAD_DIM // 4
SGU_CHUNK = 128
SGU_GROUPS = 16
PEER_HEADS = 8
PEER_NKEYS = 128
PEER_TOPK = 16
PEER_HALF = 128
N_MOD = 6
EPS = 1e-6
N_MIXERS = 3
N_CTX_TOKENS = 16 * 256

LANES = 128
VMEM_LIMIT = 56 * 1024 * 1024

SEQ_ALIGN = 1024
MOD_ROWS = 8
TM = 1024
TN = 512
TR_NM = 256
TR_ROUTE = 256
TM_PEER = 1024
A_STEP = 4
E_STEP = A_STEP * PEER_NKEYS


def _cparams(sem):
    return pltpu.CompilerParams(dimension_semantics=sem, vmem_limit_bytes=VMEM_LIMIT)


def _mod_row(tile_idx, tile_rows):
    per = SEQ_ALIGN // tile_rows
    return jnp.maximum(tile_idx // per - (N_CTX_TOKENS // SEQ_ALIGN - 1), 0)


def _gelu(x):
    c = math.sqrt(2.0 / math.pi)
    return 0.5 * x * (1.0 + jnp.tanh(c * (x + 0.044715 * (x * x * x))))


def _mod_kernel(c_ref, w_ref, b_ref, o_ref):
    c = c_ref[...]
    s = c * (1.0 / (1.0 + jnp.exp(-c)))
    o_ref[...] = jnp.dot(s, w_ref[...], preferred_element_type=F32) + b_ref[...]


def _modulation(cvec, w_mod, b_mod):
    depth, d, n = w_mod.shape
    tn = 1024
    return pl.pallas_call(
        _mod_kernel,
        out_shape=jax.ShapeDtypeStruct((depth, MOD_ROWS, n), F32),
        grid=(depth, n // tn),
        in_specs=[
            pl.BlockSpec((MOD_ROWS, d), lambda l, j: (0, 0)),
            pl.BlockSpec((None, d, tn), lambda l, j: (l, 0, j)),
            pl.BlockSpec((None, 1, tn), lambda l, j: (l, 0, j)),
        ],
        out_specs=pl.BlockSpec((None, MOD_ROWS, tn), lambda l, j: (l, 0, j)),
        compiler_params=_cparams(("parallel", "parallel")),
        name="modulation",
    )(cvec, w_mod, b_mod.reshape(depth, 1, n))


def _norm_mod_kernel(has_res, x_ref, *refs):
    if has_res:
        y_ref, gate_ref, g_ref, shift_ref, scale_ref, xo_ref, h_ref = refs
        x = x_ref[...] + gate_ref[...] * y_ref[...]
        xo_ref[...] = x
    else:
        g_ref, shift_ref, scale_ref, h_ref = refs
        x = x_ref[...]
    ms = jnp.mean(x * x, axis=-1, keepdims=True)
    y = x * lax.rsqrt(ms + EPS)
    y = y * g_ref[...]
    h_ref[...] = (y * (1.0 + scale_ref[...]) + shift_ref[...]).astype(h_ref.dtype)


def _norm_mod(x, y, mods, gate_ln, shift_ln, scale_ln, g):
    t, d = x.shape
    tr = TR_NM
    row_spec = pl.BlockSpec((tr, d), lambda i: (i, 0))

    def mod_spec(ln):
        return pl.BlockSpec((None, None, None, 1, d),
                            lambda i: (ln[0], _mod_row(i, tr), ln[1], 0, 0))

    g_spec = pl.BlockSpec((1, d), lambda i: (0, 0))
    has_res = y is not None
    if has_res:
        args = (x, y, mods, g.reshape(1, d), mods, mods)
        in_specs = [row_spec, row_spec, mod_spec(gate_ln), g_spec, mod_spec(shift_ln), mod_spec(scale_ln)]
        out_shape = (jax.ShapeDtypeStruct((t, d), F32), jax.ShapeDtypeStruct((t, d), BF16))
        out_specs = (row_spec, row_spec)
    else:
        args = (x, g.reshape(1, d), mods, mods)
        in_specs = [row_spec, g_spec, mod_spec(shift_ln), mod_spec(scale_ln)]
        out_shape = jax.ShapeDtypeStruct((t, d), BF16)
        out_specs = row_spec
    out = pl.pallas_call(
        functools.partial(_norm_mod_kernel, has_res),
        out_shape=out_shape, grid=(t // tr,), in_specs=in_specs, out_specs=out_specs,
        compiler_params=_cparams(("parallel",)), name="norm_mod",
    )(*args)
    return out if has_res else (x, out)


def _final_norm_kernel(x_ref, y_ref, gate_ref, g_ref, o_ref):
    x = x_ref[...] + gate_ref[...] * y_ref[...]
    ms = jnp.mean(x * x, axis=-1, keepdims=True)
    o_ref[...] = x * lax.rsqrt(ms + EPS) * g_ref[...]


def _final_norm(x, y, mods, gate_ln, g):
    t, d = x.shape
    tr = TR_NM
    row_spec = pl.BlockSpec((tr, d), lambda i: (i, 0))
    return pl.pallas_call(
        _final_norm_kernel,
        out_shape=jax.ShapeDtypeStruct((t, d), F32), grid=(t // tr,),
        in_specs=[row_spec, row_spec,
                  pl.BlockSpec((None, None, None, 1, d),
                               lambda i: (gate_ln[0], _mod_row(i, tr), gate_ln[1], 0, 0)),
                  pl.BlockSpec((1, d), lambda i: (0, 0))],
        out_specs=row_spec,
        compiler_params=_cparams(("parallel",)), name="final_norm",
    )(x, y, mods, g.reshape(1, d))


def _matmul_kernel(act, a_ref, w_ref, o_ref):
    acc = jnp.dot(a_ref[...], w_ref[...].astype(BF16), preferred_element_type=F32)
    if act == "gelu":
        acc = _gelu(acc)
    o_ref[...] = acc.astype(o_ref.dtype)


def _matmul(a, w_stack, layer, out_dtype=F32, act=None):
    m, k = a.shape
    n = w_stack.shape[2]
    tm = min(TM, m)
    return pl.pallas_call(
        functools.partial(_matmul_kernel, act),
        out_shape=jax.ShapeDtypeStruct((m, n), out_dtype),
        grid=(m // tm, n // TN),
        in_specs=[pl.BlockSpec((tm, k), lambda i, j: (i, 0)),
                  pl.BlockSpec((None, k, TN), lambda i, j: (layer, 0, j))],
        out_specs=pl.BlockSpec((tm, TN), lambda i, j: (i, j)),
        compiler_params=_cparams(("parallel", "arbitrary")), name="matmul",
    )(a, w_stack)


def _rope_tables(seq_len):
    rows = seq_len // GRID_W
    t = jnp.arange(seq_len)
    r = jnp.repeat(jnp.arange(rows), GRID_W).astype(F32)
    col = (t % GRID_W).astype(F32)
    inv = ROPE_THETA ** (-jnp.arange(AXIS_PAIRS, dtype=F32) / AXIS_PAIRS)
    ang = jnp.concatenate([r[:, None] * inv, col[:, None] * inv], axis=-1)
    cos, sin = jnp.cos(ang), jnp.sin(ang)
    cos_f = jnp.repeat(cos, 2, axis=-1)
    sin_s = jnp.stack([-sin, sin], axis=-1).reshape(seq_len, HEAD_DIM)
    return cos_f, sin_s


def _qk_post_kernel(n_ctx_tiles, qkv_ref, g_ref, cos_ref, sin_ref, q_ref, k_ref):
    is_lat = pl.program_id(0) >= n_ctx_tiles
    cos = jnp.where(is_lat, cos_ref[...], 1.0)
    sin = jnp.where(is_lat, sin_ref[...], 0.0)
    lane = lax.broadcasted_iota(I32, cos.shape, 1)
    even = (lane % 2) == 0
    for hd in range(N_HEADS + N_KV_HEADS):
        x = qkv_ref[:, hd * HEAD_DIM:(hd + 1) * HEAD_DIM]
        ms = jnp.mean(x * x, axis=-1, keepdims=True)
        y = x * lax.rsqrt(ms + EPS) * g_ref[:, hd * HEAD_DIM:(hd + 1) * HEAD_DIM]
        partner = jnp.where(even, pltpu.roll(y, HEAD_DIM - 1, 1), pltpu.roll(y, 1, 1))
        y = y * cos + partner * sin
        if hd < N_HEADS:
            q_ref[:, hd * HEAD_DIM:(hd + 1) * HEAD_DIM] = y.astype(q_ref.dtype)
        else:
            kh = hd - N_HEADS
            k_ref[:, kh * HEAD_DIM:(kh + 1) * HEAD_DIM] = y


def _qk_post(qkv, g_full, cos_f, sin_s, n_ctx_tokens):
    t = qkv.shape[0]
    tr = 256
    nq, nk = N_HEADS * HEAD_DIM, N_KV_HEADS * HEAD_DIM
    lat_tiles = cos_f.shape[0] // tr
    n_ctx_tiles = n_ctx_tokens // tr

    def pos_map(i):
        return (jnp.maximum(i - n_ctx_tiles, 0) % lat_tiles, 0)

    return pl.pallas_call(
        functools.partial(_qk_post_kernel, n_ctx_tiles),
        out_shape=(jax.ShapeDtypeStruct((t, nq), BF16), jax.ShapeDtypeStruct((t, nk), F32)),
        grid=(t // tr,),
        in_specs=[pl.BlockSpec((tr, nq + nk), lambda i: (i, 0)),
                  pl.BlockSpec((1, nq + nk), lambda i: (0, 0)),
                  pl.BlockSpec((tr, HEAD_DIM), pos_map),
                  pl.BlockSpec((tr, HEAD_DIM), pos_map)],
        out_specs=(pl.BlockSpec((tr, nq), lambda i: (i, 0)), pl.BlockSpec((tr, nk), lambda i: (i, 0))),
        compiler_params=_cparams(("parallel",)), name="qk_post",
    )(qkv, g_full, cos_f, sin_s)


def _attn_kernel(has_cache, q_ref, k_ref, v_ref, *refs):
    if has_cache:
        ck_ref, cv_ref, o_ref = refs
    else:
        (o_ref,) = refs
    scale = HEAD_DIM ** -0.5
    k = k_ref[...].astype(BF16)
    v = v_ref[...].astype(BF16)
    if has_cache:
        ck = ck_ref[...].astype(BF16)
        cv = cv_ref[...].astype(BF16)
    nt = (((1,), (1,)), ((), ()))
    for g in range(KV_GROUP):
        q = q_ref[:, g * HEAD_DIM:(g + 1) * HEAD_DIM]
        s1 = lax.dot_general(q, k, nt, preferred_element_type=F32) * scale
        m = jnp.max(s1, axis=-1, keepdims=True)
        if has_cache:
            s2 = lax.dot_general(q, ck, nt, preferred_element_type=F32) * scale
            m = jnp.maximum(m, jnp.max(s2, axis=-1, keepdims=True))
        p1 = jnp.exp(s1 - m)
        l = jnp.sum(p1, axis=-1, keepdims=True)
        if has_cache:
            p2 = jnp.exp(s2 - m)
            l = l + jnp.sum(p2, axis=-1, keepdims=True)
        inv = 1.0 / l
        o = jnp.dot((p1 * inv).astype(BF16), v, preferred_element_type=F32)
        if has_cache:
            o = o + jnp.dot((p2 * inv).astype(BF16), cv, preferred_element_type=F32)
        o_ref[:, g * HEAD_DIM:(g + 1) * HEAD_DIM] = o.astype(o_ref.dtype)


def _attention(qn, kn, qkv, row0, n_seq, seq_len, cache_k=None, cache_v=None):
    tq = 256
    qb = seq_len // tq
    r0q = row0 // tq
    r0k = row0 // seq_len
    gw = KV_GROUP * HEAD_DIM
    v_col0 = (N_HEADS + N_KV_HEADS)
    has_cache = cache_k is not None
    in_specs = [
        pl.BlockSpec((tq, gw), lambda b, h, i: (r0q + b * qb + i, h)),
        pl.BlockSpec((seq_len, HEAD_DIM), lambda b, h, i: (r0k + b, h)),
        pl.BlockSpec((seq_len, HEAD_DIM), lambda b, h, i: (r0k + b, v_col0 + h)),
    ]
    args = [qn, kn, qkv]
    if has_cache:
        past = cache_k.shape[2]
        cspec = pl.BlockSpec((None, None, past, HEAD_DIM), lambda b, h, i: (b, h, 0, 0))
        in_specs += [cspec, cspec]
        args += [cache_k, cache_v]
    return pl.pallas_call(
        functools.partial(_attn_kernel, has_cache),
        out_shape=jax.ShapeDtypeStruct((n_seq * seq_len, N_HEADS * HEAD_DIM), BF16),
        grid=(n_seq, N_KV_HEADS, qb),
        in_specs=in_specs,
        out_specs=pl.BlockSpec((tq, gw), lambda b, h, i: (b * qb + i, h)),
        compiler_params=_cparams(("parallel", "parallel", "arbitrary")), name="attention",
    )(*args)


def _sgu_mid_kernel(u_ref, v_ref, g_ref, ws_ref, b_ref, o_ref):
    v = v_ref[...]
    ms = jnp.mean(v * v, axis=-1, keepdims=True)
    vn = (v * lax.rsqrt(ms + EPS) * g_ref[...]).astype(BF16)
    for g in range(SGU_GROUPS):
        sl = slice(g * LANES, (g + 1) * LANES)
        mixed = jnp.dot(ws_ref[g].astype(BF16), vn[:, sl], preferred_element_type=F32) + b_ref[:, sl]
        o_ref[:, sl] = (u_ref[:, sl] * mixed).astype(o_ref.dtype)


def _sgu_mid(z, g_v, w_s, b_full):
    t = z.shape[0]
    w = z.shape[1] // 2
    ch = SGU_CHUNK
    return pl.pallas_call(
        _sgu_mid_kernel,
        out_shape=jax.ShapeDtypeStruct((t, w), BF16),
        grid=(t // ch,),
        in_specs=[pl.BlockSpec((ch, w), lambda i: (i, 0)),
                  pl.BlockSpec((ch, w), lambda i: (i, 1)),
                  pl.BlockSpec((1, w), lambda i: (0, 0)),
                  pl.BlockSpec((SGU_GROUPS, ch, ch), lambda i: (0, 0, 0)),
                  pl.BlockSpec((ch, w), lambda i: (0, 0))],
        out_specs=pl.BlockSpec((ch, w), lambda i: (i, 0)),
        compiler_params=_cparams(("parallel",)), name="sgu_mid",
    )(z, z, g_v.reshape(1, w), w_s, b_full)


def _conv_mid_kernel(n_ctx_tiles, ctx_len, lat_len, bg_ref, cg_ref, z_ref, w_ref, o_ref):
    z = cg_ref[...] * z_ref[...]
    rows = z.shape[0]
    seq_len = jnp.where(pl.program_id(0) < n_ctx_tiles, ctx_len, lat_len)
    pos = lax.broadcasted_iota(I32, z.shape, 0) % seq_len
    prev = jnp.where(pos == 0, 0.0, pltpu.roll(z, 1, 0))
    nxt = jnp.where(pos == seq_len - 1, 0.0, pltpu.roll(z, rows - 1, 0))
    zc = w_ref[0:1, :] * prev + w_ref[1:2, :] * z + w_ref[2:3, :] * nxt
    o_ref[...] = (bg_ref[...] * zc).astype(o_ref.dtype)


def _conv_mid(y, conv_w, n_ctx_tokens, ctx_len, lat_len):
    t = y.shape[0]
    d = y.shape[1] // 3
    tr, tc = SEQ_ALIGN, 512
    nc = d // tc
    return pl.pallas_call(
        functools.partial(_conv_mid_kernel, n_ctx_tokens // tr, ctx_len, lat_len),
        out_shape=jax.ShapeDtypeStruct((t, d), BF16),
        grid=(t // tr, nc),
        in_specs=[pl.BlockSpec((tr, tc), lambda i, j: (i, j)),
                  pl.BlockSpec((tr, tc), lambda i, j: (i, nc + j)),
                  pl.BlockSpec((tr, tc), lambda i, j: (i, 2 * nc + j)),
                  pl.BlockSpec((8, tc), lambda i, j: (0, j))],
        out_specs=pl.BlockSpec((tr, tc), lambda i, j: (i, j)),
        compiler_params=_cparams(("parallel", "parallel")), name="conv_mid",
    )(y, y, y, conv_w)


_CAND = [(i, j) for i in range(PEER_TOPK) for j in range(PEER_TOPK) if (i + 1) * (j + 1) <= PEER_TOPK]


def _top_list(s, k, exact_ties):
    n = s.shape[0]
    if exact_ties:
        iota = lax.broadcasted_iota(I32, s.shape, 0)
    vals = []
    for r in range(k):
        m = jnp.max(s, axis=0, keepdims=True)
        hit = s == m
        if exact_ties:
            idx = jnp.min(jnp.where(hit, iota, n), axis=0, keepdims=True)
            hit = iota == idx
        s = jnp.where(hit, -_TAKEN * (1.0 + r / k), s)
        vals.append(m)
    rank = jnp.where(s < -0.5 * _TAKEN, (s * (-1.0 / _TAKEN) - 1.0) * k, float(k))
    return vals, rank


_TAKEN = 2.0 ** 126


def _ranked_count_off(rank, k):
    return jnp.sum(jnp.where(rank < k, 1, 0), axis=0, keepdims=True) != k


def _bf16_pair_word(x):
    hi = pltpu.bitcast(x.astype(BF16).astype(F32), I32)
    return hi | lax.shift_right_logical(hi, 16)


def _route_head(h, exact_halves, q_ref, keys_ref, len_ref, e1_ref, r2_ref, e2_ref):
    nt = (((1,), (1,)), ((), ()))
    k = PEER_TOPK
    sc = []
    for c in range(2):
        col = (2 * h + c) * PEER_HALF
        sc.append(lax.dot_general(keys_ref[h, c].astype(BF16), q_ref[:, col:col + PEER_HALF], nt,
                                  preferred_element_type=F32))
    v1, rank1 = _top_list(sc[0], k, exact_halves)
    v2, rank2 = _top_list(sc[1], k, exact_halves)
    cand = jnp.concatenate([v1[i] + v2[j] for (i, j) in _CAND], axis=0)
    _, crank = _top_list(cand, k, True)
    sel = crank < k
    ev1 = [jnp.exp(v - v1[0]) for v in v1]
    ev2 = [jnp.exp(v - v2[0]) for v in v2]
    zsum = jnp.zeros_like(v1[0])
    row_len = [jnp.zeros(v1[0].shape, F32) for _ in range(k)]
    for r, (i, j) in enumerate(_CAND):
        s_r = sel[r:r + 1, :]
        zsum = zsum + jnp.where(s_r, ev1[i] * ev2[j], 0.0)
        row_len[i] = row_len[i] + jnp.where(s_r, 1.0, 0.0)
    len_a = jnp.zeros(rank1.shape, F32)
    for i in range(k):
        len_a = jnp.where(rank1 == i, row_len[i], len_a)
    len_ref[:, h, :] = _bf16_pair_word(len_a)
    e1_ref[:, h, :] = _bf16_pair_word(jnp.exp(sc[0] - v1[0]))
    r2_ref[h] = pltpu.bitcast(rank2.astype(BF16), I32)
    e2_ref[h] = pltpu.bitcast((jnp.exp(sc[1] - v2[0]) / zsum).astype(BF16), I32)
    if exact_halves:
        return None
    return _ranked_count_off(rank1, k) | _ranked_count_off(rank2, k)


def _route_kernel(q_ref, keys_ref, *out_refs):
    tie = None
    for h in range(PEER_HEADS):
        t = _route_head(h, False, q_ref, keys_ref, *out_refs)
        tie = t if tie is None else tie | t

    @pl.when(jnp.max(jnp.where(tie, 1, 0)) > 0)
    def _():
        for h in range(PEER_HEADS):
            _route_head(h, True, q_ref, keys_ref, *out_refs)


def _route(q, keys_stack, layer):
    t = q.shape[0]
    tr = TR_ROUTE
    shp1 = (PEER_NKEYS, PEER_HEADS, t)
    spec1 = pl.BlockSpec((PEER_NKEYS, PEER_HEADS, tr), lambda i: (0, 0, i))
    shp2 = (PEER_HEADS, PEER_NKEYS // 2, t)
    spec2 = pl.BlockSpec((PEER_HEADS, PEER_NKEYS // 2, tr), lambda i: (0, 0, i))
    return pl.pallas_call(
        _route_kernel,
        out_shape=(jax.ShapeDtypeStruct(shp1, I32), jax.ShapeDtypeStruct(shp1, I32),
                   jax.ShapeDtypeStruct(shp2, I32), jax.ShapeDtypeStruct(shp2, I32)),
        grid=(t // tr,),
        in_specs=[pl.BlockSpec((tr, q.shape[1]), lambda i: (i, 0)),
                  pl.BlockSpec((None,) + keys_stack.shape[1:], lambda i: (layer, 0, 0, 0, 0))],
        out_specs=(spec1, spec1, spec2, spec2),
        compiler_params=_cparams(("parallel",)), name="peer_route",
    )(q, keys_stack)


def _gelu_gate(x, g):
    c = 2.0 * math.sqrt(2.0 / math.pi) * math.log2(math.e)
    e = jnp.exp2(x * (-c - (c * 0.044715) * (x * x)))
    return (x * g) / (1.0 + e)


def _bf16_rows(word_row):
    words = jnp.broadcast_to(word_row, (PEER_NKEYS // 2, word_row.shape[1]))
    return pltpu.bitcast(words, BF16)


def _peer_dense_kernel(n_steps, h_ref, len_ref, e1_ref, r2_ref, e2_ref, u_ref, v_ref, y_ref, s0, s1, ga):
    step = pl.program_id(0)
    nt = (((1,), (1,)), ((), ()))
    tm = h_ref.shape[0]
    d = v_ref.shape[1]
    half_e = E_STEP // 2

    @pl.when(step == 0)
    def _():
        s0[...] = jnp.zeros_like(s0)
        s1[...] = jnp.zeros_like(s1)

    @pl.when((step == 0) | (lax.rem(step + n_steps - 1, n_steps) == 0))
    def _():
        y_ref[...] = jnp.zeros_like(y_ref)

    def body(s_new, s_old):
        def scores(half):
            rows = slice(half * half_e, (half + 1) * half_e)
            s_new[:, rows] = lax.dot_general(h_ref[...], u_ref[rows, :].astype(BF16), nt,
                                             preferred_element_type=F32)

        def gates(r, c):
            tok = slice(c * LANES, (c + 1) * LANES)
            gt = None
            for hd in range(PEER_HEADS):
                r2 = pltpu.bitcast(r2_ref[hd, :, tok], BF16)
                e2 = pltpu.bitcast(e2_ref[hd, :, tok], BF16)
                hit = r2 < _bf16_rows(len_ref[r, hd:hd + 1, tok])
                term = jnp.where(hit, e2, 0.0) * _bf16_rows(e1_ref[r, hd:hd + 1, tok])
                gt = term if gt is None else gt + term
            cols = slice(r * PEER_NKEYS, (r + 1) * PEER_NKEYS)
            ga[tok, cols] = _gelu_gate(s_old[tok, cols], gt.astype(F32).T).astype(BF16)

        def contract(piece, n_pieces):
            w = d // n_pieces
            cols = slice(piece * w, (piece + 1) * w)
            y_ref[:, cols] += jnp.dot(ga[...], v_ref[:, cols].astype(BF16), preferred_element_type=F32)

        scores(0)
        scores(1)
        for r in range(A_STEP):
            for c in range(tm // LANES):
                gates(r, c)
        for piece in range(4):
            contract(piece, 4)

    even = lax.rem(step, 2) == 0

    @pl.when(even)
    def _():
        body(s0, s1)

    @pl.when(jnp.logical_not(even))
    def _():
        body(s1, s0)


def _peer_dense(h, route, u_stack, v_stack, layer):
    t, d = h.shape
    e = u_stack.shape[1]
    tm = min(TM_PEER, t)
    n_steps = e // E_STEP
    len_a, e1, r2, e2 = route
    once = pl.Buffered(1)
    n_pairs = (t // tm) * n_steps

    def pair(g, lag):
        p = jnp.clip(g - lag, 0, n_pairs - 1)
        return p // n_steps, p % n_steps

    aspec = pl.BlockSpec((A_STEP, PEER_HEADS, tm), lambda g: (pair(g, 1)[1], 0, pair(g, 1)[0]))
    bspec = pl.BlockSpec((PEER_HEADS, PEER_NKEYS // 2, tm), lambda g: (0, 0, pair(g, 1)[0]), pipeline_mode=once)
    return pl.pallas_call(
        functools.partial(_peer_dense_kernel, n_steps),
        out_shape=jax.ShapeDtypeStruct((t, d), F32),
        grid=(n_pairs + 1,),
        in_specs=[pl.BlockSpec((tm, d), lambda g: (pair(g, 0)[0], 0), pipeline_mode=once),
                  aspec, aspec, bspec, bspec,
                  pl.BlockSpec((None, E_STEP, d), lambda g: (layer, pair(g, 0)[1], 0)),
                  pl.BlockSpec((None, E_STEP, d), lambda g: (layer, pair(g, 1)[1], 0))],
        out_specs=pl.BlockSpec((tm, d), lambda g: (pair(g, 1)[0], 0)),
        scratch_shapes=[pltpu.VMEM((tm, E_STEP), F32), pltpu.VMEM((tm, E_STEP), F32),
                        pltpu.VMEM((tm, E_STEP), BF16)],
        compiler_params=_cparams(("arbitrary",)), name="peer_dense",
    )(h, len_a, e1, r2, e2, u_stack, v_stack)


def _peer(h, w_q, keys, u_stack, v_stack, layer):
    q = _matmul(h, w_q, layer, out_dtype=BF16)
    return _peer_dense(h, _route(q, keys, layer), u_stack, v_stack, layer)


def kernel(x_prompt, x_sample, cache_k, cache_v, c, c_ctx, norm_g, final_g, w_mod, b_mod, attn_w_qkv, attn_q_g, attn_k_g, attn_w_o, sgu_w_in, sgu_g, sgu_w_s, sgu_b, sgu_w_out, conv_w_in, conv_w, conv_w_out, peer_w_q, peer_keys, peer_u, peer_v):
    nb, sl, d = x_prompt.shape
    db, dsl, _ = x_sample.shape
    depth = w_mod.shape[0]
    n_ctx = nb * sl
    assert d == D_MODEL and dsl == SEQ_ALIGN and n_ctx == N_CTX_TOKENS and SEQ_ALIGN % sl == 0
    assert 1 + db <= MOD_ROWS

    x = jnp.concatenate([x_prompt.reshape(n_ctx, d), x_sample.reshape(db * dsl, d)], axis=0)
    cvec = jnp.concatenate([c_ctx[None, :], c, jnp.zeros((MOD_ROWS - 1 - db, d), F32)], axis=0)
    mods = _modulation(cvec, w_mod, b_mod).reshape(depth, MOD_ROWS, N_MOD, 1, d)

    cos_f, sin_s = _rope_tables(dsl)
    nq, nk = N_HEADS * HEAD_DIM, N_KV_HEADS * HEAD_DIM
    new_k, new_v = [], []
    y = None
    for i in range(depth):
        kind, j = i % N_MIXERS, i // N_MIXERS
        x, h = _norm_mod(x, y, mods, (i - 1, 5), (i, 0), (i, 1), norm_g[i, 0])
        if kind == 0:
            qkv = _matmul(h, attn_w_qkv, j)
            g_full = jnp.concatenate([jnp.tile(attn_q_g[j], N_HEADS), jnp.tile(attn_k_g[j], N_KV_HEADS)])
            qn, kn = _qk_post(qkv, g_full.reshape(1, nq + nk), cos_f, sin_s, n_ctx)
            new_k.append(kn[:n_ctx].reshape(nb, sl, N_KV_HEADS, HEAD_DIM))
            new_v.append(qkv[:n_ctx, nq + nk:].reshape(nb, sl, N_KV_HEADS, HEAD_DIM))
            o_ctx = _attention(qn, kn, qkv, 0, nb, sl)
            ck = jnp.transpose(cache_k[:, j], (0, 2, 1, 3))
            cv = jnp.transpose(cache_v[:, j], (0, 2, 1, 3))
            o_lat = _attention(qn, kn, qkv, n_ctx, db, dsl, ck, cv)
            o = _matmul(jnp.concatenate([o_ctx, o_lat], axis=0), attn_w_o, j)
        elif kind == 1:
            z = _matmul(h, sgu_w_in, j, act="gelu")
            b_full = jnp.repeat(sgu_b[j].T, SGU_CHUNK, axis=1)
            o = _matmul(_sgu_mid(z, sgu_g[j], sgu_w_s[j], b_full), sgu_w_out, j)
        else:
            yc = _matmul(h, conv_w_in, j)
            cw = jnp.concatenate([conv_w[j], jnp.zeros((8 - conv_w.shape[1], d), F32)], axis=0)
            o = _matmul(_conv_mid(yc, cw, n_ctx, sl, dsl), conv_w_out, j)
        x, h = _norm_mod(x, o, mods, (i, 2), (i, 3), (i, 4), norm_g[i, 1])
        y = _peer(h, peer_w_q, peer_keys, peer_u, peer_v, i)

    out = _final_norm(x, y, mods, (depth - 1, 5), final_g)
    y_prompt = out[:n_ctx].reshape(nb, sl, d)
    y_sample = out[n_ctx:].reshape(db, dsl, d)
    return (y_prompt, y_sample, jnp.stack(new_k, axis=1), jnp.stack(new_v, axis=1))
```

```python
import functools
import math

import jax
import jax.numpy as jnp
from jax import lax
from jax.experimental import pallas as pl
from jax.experimental.pallas import tpu as pltpu

F32 = jnp.float32
BF16 = jnp.bfloat16
I32 = jnp.int32

D_MODEL = 2048
HEAD_DIM = 128
N_HEADS = 16
N_KV_HEADS = 4
KV_GROUP = N_HEADS // N_KV_HEADS
GRID_W = 64
ROPE_THETA = 10000.0
AXIS_PAIRS = HEAD_DIM // 4
SGU_CHUNK = 128
SGU_GROUPS = 16
PEER_HEADS = 8
PEER_NKEYS = 128
PEER_TOPK = 16
PEER_HALF = 128
N_MOD = 6
EPS = 1e-6
N_MIXERS = 3
N_CTX_TOKENS = 16 * 256

LANES = 128
VMEM_LIMIT = 56 * 1024 * 1024

SEQ_ALIGN = 1024
MOD_ROWS = 8
TM = 1024
TN = 512
TR_NM = 256
TM_NORM_MM = 512
TR_ROUTE = 256
TM_PEER = 1024
A_STEP = 4
E_STEP = A_STEP * PEER_NKEYS


def _cparams(sem):
    return pltpu.CompilerParams(dimension_semantics=sem, vmem_limit_bytes=VMEM_LIMIT)


def _mod_row(tile_idx, tile_rows):
    per = SEQ_ALIGN // tile_rows
    return jnp.maximum(tile_idx // per - (N_CTX_TOKENS // SEQ_ALIGN - 1), 0)


def _gelu(x):
    c = math.sqrt(2.0 / math.pi)
    return 0.5 * x * (1.0 + jnp.tanh(c * (x + 0.044715 * (x * x * x))))


def _mod_kernel(c_ref, w_ref, b_ref, o_ref):
    c = c_ref[...]
    s = c * (1.0 / (1.0 + jnp.exp(-c)))
    o_ref[...] = jnp.dot(s, w_ref[...], preferred_element_type=F32) + b_ref[...]


def _modulation(cvec, w_mod, b_mod):
    depth, d, n = w_mod.shape
    tn = 1024
    return pl.pallas_call(
        _mod_kernel,
        out_shape=jax.ShapeDtypeStruct((depth, MOD_ROWS, n), F32),
        grid=(depth, n // tn),
        in_specs=[
            pl.BlockSpec((MOD_ROWS, d), lambda l, j: (0, 0)),
            pl.BlockSpec((None, d, tn), lambda l, j: (l, 0, j)),
            pl.BlockSpec((None, 1, tn), lambda l, j: (l, 0, j)),
        ],
        out_specs=pl.BlockSpec((None, MOD_ROWS, tn), lambda l, j: (l, 0, j)),
        compiler_params=_cparams(("parallel", "parallel")),
        name="modulation",
    )(cvec, w_mod, b_mod.reshape(depth, 1, n))


def _norm_matmul_kernel(has_res, want_h, act, x_ref, *refs):
    refs = list(refs)
    if has_res:
        y_ref, gate_ref = refs[:2]
        refs = refs[2:]
    g_ref, shift_ref, scale_ref, w_ref = refs[:4]
    refs = refs[4:]
    if has_res:
        xo_ref = refs.pop(0)
    if want_h:
        ho_ref = refs.pop(0)
    o_ref, h_scr = refs

    @pl.when(pl.program_id(1) == 0)
    def _():
        x = x_ref[...]
        if has_res:
            x = x + gate_ref[...] * y_ref[...]
            xo_ref[...] = x
        ms = jnp.mean(x * x, axis=-1, keepdims=True)
        y = x * lax.rsqrt(ms + EPS)
        y = y * g_ref[...]
        h = (y * (1.0 + scale_ref[...]) + shift_ref[...]).astype(BF16)
        h_scr[...] = h
        if want_h:
            ho_ref[...] = h

    acc = jnp.dot(h_scr[...], w_ref[...].astype(BF16), preferred_element_type=F32)
    if act == "gelu":
        acc = _gelu(acc)
    o_ref[...] = acc.astype(o_ref.dtype)


def _norm_matmul(x, y, mods, gate_ln, shift_ln, scale_ln, g, w_stack, layer, want_h=False, out_dtype=F32,
                 act=None):
    t, d = x.shape
    n = w_stack.shape[2]
    tm = TM_NORM_MM
    row_spec = pl.BlockSpec((tm, d), lambda i, j: (i, 0))

    def mod_spec(ln):
        return pl.BlockSpec((None, None, None, 1, d),
                            lambda i, j: (ln[0], _mod_row(i, tm), ln[1], 0, 0))

    has_res = y is not None
    args, in_specs = [x], [row_spec]
    if has_res:
        args += [y, mods]
        in_specs += [row_spec, mod_spec(gate_ln)]
    args += [g.reshape(1, d), mods, mods, w_stack]
    in_specs += [pl.BlockSpec((1, d), lambda i, j: (0, 0)), mod_spec(shift_ln), mod_spec(scale_ln),
                 pl.BlockSpec((None, d, TN), lambda i, j: (layer, 0, j))]
    out_shape, out_specs = [], []
    if has_res:
        out_shape.append(jax.ShapeDtypeStruct((t, d), F32))
        out_specs.append(row_spec)
    if want_h:
        out_shape.append(jax.ShapeDtypeStruct((t, d), BF16))
        out_specs.append(row_spec)
    out_shape.append(jax.ShapeDtypeStruct((t, n), out_dtype))
    out_specs.append(pl.BlockSpec((tm, TN), lambda i, j: (i, j)))
    outs = list(pl.pallas_call(
        functools.partial(_norm_matmul_kernel, has_res, want_h, act),
        out_shape=tuple(out_shape), grid=(t // tm, n // TN), in_specs=in_specs, out_specs=tuple(out_specs),
        scratch_shapes=[pltpu.VMEM((tm, d), BF16)],
        compiler_params=_cparams(("parallel", "arbitrary")), name="norm_matmul",
    )(*args))
    x_new = outs.pop(0) if has_res else x
    h = outs.pop(0) if want_h else None
    return x_new, h, outs[0]


def _final_norm_kernel(x_ref, y_ref, gate_ref, g_ref, o_ref):
    x = x_ref[...] + gate_ref[...] * y_ref[...]
    ms = jnp.mean(x * x, axis=-1, keepdims=True)
    o_ref[...] = x * lax.rsqrt(ms + EPS) * g_ref[...]


def _final_norm(x, y, mods, gate_ln, g):
    t, d = x.shape
    tr = TR_NM
    row_spec = pl.BlockSpec((tr, d), lambda i: (i, 0))
    return pl.pallas_call(
        _final_norm_kernel,
        out_shape=jax.ShapeDtypeStruct((t, d), F32), grid=(t // tr,),
        in_specs=[row_spec, row_spec,
                  pl.BlockSpec((None, None, None, 1, d),
                               lambda i: (gate_ln[0], _mod_row(i, tr), gate_ln[1], 0, 0)),
                  pl.BlockSpec((1, d), lambda i: (0, 0))],
        out_specs=row_spec,
        compiler_params=_cparams(("parallel",)), name="final_norm",
    )(x, y, mods, g.reshape(1, d))


def _matmul_kernel(act, a_ref, w_ref, o_ref):
    acc = jnp.dot(a_ref[...], w_ref[...].astype(BF16), preferred_element_type=F32)
    if act == "gelu":
        acc = _gelu(acc)
    o_ref[...] = acc.astype(o_ref.dtype)


def _matmul(a, w_stack, layer, out_dtype=F32, act=None):
    m, k = a.shape
    n = w_stack.shape[2]
    tm = min(TM, m)
    return pl.pallas_call(
        functools.partial(_matmul_kernel, act),
        out_shape=jax.ShapeDtypeStruct((m, n), out_dtype),
        grid=(m // tm, n // TN),
        in_specs=[pl.BlockSpec((tm, k), lambda i, j: (i, 0)),
                  pl.BlockSpec((None, k, TN), lambda i, j: (layer, 0, j))],
        out_specs=pl.BlockSpec((tm, TN), lambda i, j: (i, j)),
        compiler_params=_cparams(("parallel", "arbitrary")), name="matmul",
    )(a, w_stack)


def _rope_tables(seq_len):
    rows = seq_len // GRID_W
    t = jnp.arange(seq_len)
    r = jnp.repeat(jnp.arange(rows), GRID_W).astype(F32)
    col = (t % GRID_W).astype(F32)
    inv = ROPE_THETA ** (-jnp.arange(AXIS_PAIRS, dtype=F32) / AXIS_PAIRS)
    ang = jnp.concatenate([r[:, None] * inv, col[:, None] * inv], axis=-1)
    cos, sin = jnp.cos(ang), jnp.sin(ang)
    cos_f = jnp.repeat(cos, 2, axis=-1)
    sin_s = jnp.stack([-sin, sin], axis=-1).reshape(seq_len, HEAD_DIM)
    return cos_f, sin_s


def _qk_post_kernel(n_ctx_tiles, qkv_ref, g_ref, cos_ref, sin_ref, q_ref, k_ref):
    is_lat = pl.program_id(0) >= n_ctx_tiles
    cos = jnp.where(is_lat, cos_ref[...], 1.0)
    sin = jnp.where(is_lat, sin_ref[...], 0.0)
    lane = lax.broadcasted_iota(I32, cos.shape, 1)
    even = (lane % 2) == 0
    for hd in range(N_HEADS + N_KV_HEADS):
        x = qkv_ref[:, hd * HEAD_DIM:(hd + 1) * HEAD_DIM]
        ms = jnp.mean(x * x, axis=-1, keepdims=True)
        y = x * lax.rsqrt(ms + EPS) * g_ref[:, hd * HEAD_DIM:(hd + 1) * HEAD_DIM]
        partner = jnp.where(even, pltpu.roll(y, HEAD_DIM - 1, 1), pltpu.roll(y, 1, 1))
        y = y * cos + partner * sin
        if hd < N_HEADS:
            q_ref[:, hd * HEAD_DIM:(hd + 1) * HEAD_DIM] = y.astype(q_ref.dtype)
        else:
            kh = hd - N_HEADS
            k_ref[:, kh * HEAD_DIM:(kh + 1) * HEAD_DIM] = y


def _qk_post(qkv, g_full, cos_f, sin_s, n_ctx_tokens):
    t = qkv.shape[0]
    tr = 256
    nq, nk = N_HEADS * HEAD_DIM, N_KV_HEADS * HEAD_DIM
    lat_tiles = cos_f.shape[0] // tr
    n_ctx_tiles = n_ctx_tokens // tr

    def pos_map(i):
        return (jnp.maximum(i - n_ctx_tiles, 0) % lat_tiles, 0)

    return pl.pallas_call(
        functools.partial(_qk_post_kernel, n_ctx_tiles),
        out_shape=(jax.ShapeDtypeStruct((t, nq), BF16), jax.ShapeDtypeStruct((t, nk), F32)),
        grid=(t // tr,),
        in_specs=[pl.BlockSpec((tr, nq + nk), lambda i: (i, 0)),
                  pl.BlockSpec((1, nq + nk), lambda i: (0, 0)),
                  pl.BlockSpec((tr, HEAD_DIM), pos_map),
                  pl.BlockSpec((tr, HEAD_DIM), pos_map)],
        out_specs=(pl.BlockSpec((tr, nq), lambda i: (i, 0)), pl.BlockSpec((tr, nk), lambda i: (i, 0))),
        compiler_params=_cparams(("parallel",)), name="qk_post",
    )(qkv, g_full, cos_f, sin_s)


def _attn_kernel(has_cache, q_ref, k_ref, v_ref, *refs):
    if has_cache:
        ck_ref, cv_ref, o_ref = refs
    else:
        (o_ref,) = refs
    scale = HEAD_DIM ** -0.5
    k = k_ref[...].astype(BF16)
    v = v_ref[...].astype(BF16)
    if has_cache:
        ck = ck_ref[...].astype(BF16)
        cv = cv_ref[...].astype(BF16)
    nt = (((1,), (1,)), ((), ()))
    for g in range(KV_GROUP):
        q = q_ref[:, g * HEAD_DIM:(g + 1) * HEAD_DIM]
        s1 = lax.dot_general(q, k, nt, preferred_element_type=F32) * scale
        m = jnp.max(s1, axis=-1, keepdims=True)
        if has_cache:
            s2 = lax.dot_general(q, ck, nt, preferred_element_type=F32) * scale
            m = jnp.maximum(m, jnp.max(s2, axis=-1, keepdims=True))
        p1 = jnp.exp(s1 - m)
        l = jnp.sum(p1, axis=-1, keepdims=True)
        if has_cache:
            p2 = jnp.exp(s2 - m)
            l = l + jnp.sum(p2, axis=-1, keepdims=True)
        inv = 1.0 / l
        o = jnp.dot((p1 * inv).astype(BF16), v, preferred_element_type=F32)
        if has_cache:
            o = o + jnp.dot((p2 * inv).astype(BF16), cv, preferred_element_type=F32)
        o_ref[:, g * HEAD_DIM:(g + 1) * HEAD_DIM] = o.astype(o_ref.dtype)


def _attention(qn, kn, qkv, row0, n_seq, seq_len, cache_k=None, cache_v=None):
    tq = 256
    qb = seq_len // tq
    r0q = row0 // tq
    r0k = row0 // seq_len
    gw = KV_GROUP * HEAD_DIM
    v_col0 = (N_HEADS + N_KV_HEADS)
    has_cache = cache_k is not None
    in_specs = [
        pl.BlockSpec((tq, gw), lambda b, h, i: (r0q + b * qb + i, h)),
        pl.BlockSpec((seq_len, HEAD_DIM), lambda b, h, i: (r0k + b, h)),
        pl.BlockSpec((seq_len, HEAD_DIM), lambda b, h, i: (r0k + b, v_col0 + h)),
    ]
    args = [qn, kn, qkv]
    if has_cache:
        past = cache_k.shape[2]
        cspec = pl.BlockSpec((None, None, past, HEAD_DIM), lambda b, h, i: (b, h, 0, 0))
        in_specs += [cspec, cspec]
        args += [cache_k, cache_v]
    return pl.pallas_call(
        functools.partial(_attn_kernel, has_cache),
        out_shape=jax.ShapeDtypeStruct((n_seq * seq_len, N_HEADS * HEAD_DIM), BF16),
        grid=(n_seq, N_KV_HEADS, qb),
        in_specs=in_specs,
        out_specs=pl.BlockSpec((tq, gw), lambda b, h, i: (b * qb + i, h)),
        compiler_params=_cparams(("parallel", "parallel", "arbitrary")), name="attention",
    )(*args)


def _sgu_mid_kernel(u_ref, v_ref, g_ref, ws_ref, b_ref, o_ref):
    v = v_ref[...]
    ms = jnp.mean(v * v, axis=-1, keepdims=True)
    vn = (v * lax.rsqrt(ms + EPS) * g_ref[...]).astype(BF16)
    for g in range(SGU_GROUPS):
        sl = slice(g * LANES, (g + 1) * LANES)
        mixed = jnp.dot(ws_ref[g].astype(BF16), vn[:, sl], preferred_element_type=F32) + b_ref[:, sl]
        o_ref[:, sl] = (u_ref[:, sl] * mixed).astype(o_ref.dtype)


def _sgu_mid(z, g_v, w_s, b_full):
    t = z.shape[0]
    w = z.shape[1] // 2
    ch = SGU_CHUNK
    return pl.pallas_call(
        _sgu_mid_kernel,
        out_shape=jax.ShapeDtypeStruct((t, w), BF16),
        grid=(t // ch,),
        in_specs=[pl.BlockSpec((ch, w), lambda i: (i, 0)),
                  pl.BlockSpec((ch, w), lambda i: (i, 1)),
                  pl.BlockSpec((1, w), lambda i: (0, 0)),
                  pl.BlockSpec((SGU_GROUPS, ch, ch), lambda i: (0, 0, 0)),
                  pl.BlockSpec((ch, w), lambda i: (0, 0))],
        out_specs=pl.BlockSpec((ch, w), lambda i: (i, 0)),
        compiler_params=_cparams(("parallel",)), name="sgu_mid",
    )(z, z, g_v.reshape(1, w), w_s, b_full)


def _conv_mid_kernel(n_ctx_tiles, ctx_len, lat_len, bg_ref, cg_ref, z_ref, w_ref, o_ref):
    z = cg_ref[...] * z_ref[...]
    rows = z.shape[0]
    seq_len = jnp.where(pl.program_id(0) < n_ctx_tiles, ctx_len, lat_len)
    pos = lax.broadcasted_iota(I32, z.shape, 0) % seq_len
    prev = jnp.where(pos == 0, 0.0, pltpu.roll(z, 1, 0))
    nxt = jnp.where(pos == seq_len - 1, 0.0, pltpu.roll(z, rows - 1, 0))
    zc = w_ref[0:1, :] * prev + w_ref[1:2, :] * z + w_ref[2:3, :] * nxt
    o_ref[...] = (bg_ref[...] * zc).astype(o_ref.dtype)


def _conv_mid(y, conv_w, n_ctx_tokens, ctx_len, lat_len):
    t = y.shape[0]
    d = y.shape[1] // 3
    tr, tc = SEQ_ALIGN, 512
    nc = d // tc
    return pl.pallas_call(
        functools.partial(_conv_mid_kernel, n_ctx_tokens // tr, ctx_len, lat_len),
        out_shape=jax.ShapeDtypeStruct((t, d), BF16),
        grid=(t // tr, nc),
        in_specs=[pl.BlockSpec((tr, tc), lambda i, j: (i, j)),
                  pl.BlockSpec((tr, tc), lambda i, j: (i, nc + j)),
                  pl.BlockSpec((tr, tc), lambda i, j: (i, 2 * nc + j)),
                  pl.BlockSpec((8, tc), lambda i, j: (0, j))],
        out_specs=pl.BlockSpec((tr, tc), lambda i, j: (i, j)),
        compiler_params=_cparams(("parallel", "parallel")), name="conv_mid",
    )(y, y, y, conv_w)


_CAND = [(i, j) for i in range(PEER_TOPK) for j in range(PEER_TOPK) if (i + 1) * (j + 1) <= PEER_TOPK]


def _top_list(s, k, exact_ties):
    n = s.shape[0]
    if exact_ties:
        iota = lax.broadcasted_iota(I32, s.shape, 0)
    vals = []
    for r in range(k):
        m = jnp.max(s, axis=0, keepdims=True)
        hit = s == m
        if exact_ties:
            idx = jnp.min(jnp.where(hit, iota, n), axis=0, keepdims=True)
            hit = iota == idx
        s = jnp.where(hit, -_TAKEN * (1.0 + r / k), s)
        vals.append(m)
    rank = jnp.where(s < -0.5 * _TAKEN, (s * (-1.0 / _TAKEN) - 1.0) * k, float(k))
    return vals, rank


_TAKEN = 2.0 ** 126


def _ranked_count_off(rank, k):
    return jnp.sum(jnp.where(rank < k, 1, 0), axis=0, keepdims=True) != k


def _bf16_pair_word(x):
    hi = pltpu.bitcast(x.astype(BF16).astype(F32), I32)
    return hi | lax.shift_right_logical(hi, 16)


def _route_head(h, exact_halves, q_ref, keys_ref, len_ref, e1_ref, r2_ref, e2_ref):
    nt = (((1,), (1,)), ((), ()))
    k = PEER_TOPK
    sc = []
    for c in range(2):
        col = (2 * h + c) * PEER_HALF
        sc.append(lax.dot_general(keys_ref[h, c].astype(BF16), q_ref[:, col:col + PEER_HALF], nt,
                                  preferred_element_type=F32))
    v1, rank1 = _top_list(sc[0], k, exact_halves)
    v2, rank2 = _top_list(sc[1], k, exact_halves)
    cand = jnp.concatenate([v1[i] + v2[j] for (i, j) in _CAND], axis=0)
    _, crank = _top_list(cand, k, True)
    sel = crank < k
    ev1 = [jnp.exp(v - v1[0]) for v in v1]
    ev2 = [jnp.exp(v - v2[0]) for v in v2]
    zsum = jnp.zeros_like(v1[0])
    row_len = [jnp.zeros(v1[0].shape, F32) for _ in range(k)]
    for r, (i, j) in enumerate(_CAND):
        s_r = sel[r:r + 1, :]
        zsum = zsum + jnp.where(s_r, ev1[i] * ev2[j], 0.0)
        row_len[i] = row_len[i] + jnp.where(s_r, 1.0, 0.0)
    len_a = jnp.zeros(rank1.shape, F32)
    for i in range(k):
        len_a = jnp.where(rank1 == i, row_len[i], len_a)
    len_ref[:, h, :] = _bf16_pair_word(len_a)
    e1_ref[:, h, :] = _bf16_pair_word(jnp.exp(sc[0] - v1[0]))
    r2_ref[h] = pltpu.bitcast(rank2.astype(BF16), I32)
    e2_ref[h] = pltpu.bitcast((jnp.exp(sc[1] - v2[0]) / zsum).astype(BF16), I32)
    if exact_halves:
        return None
    return _ranked_count_off(rank1, k) | _ranked_count_off(rank2, k)


def _route_kernel(q_ref, keys_ref, *out_refs):
    tie = None
    for h in range(PEER_HEADS):
        t = _route_head(h, False, q_ref, keys_ref, *out_refs)
        tie = t if tie is None else tie | t

    @pl.when(jnp.max(jnp.where(tie, 1, 0)) > 0)
    def _():
        for h in range(PEER_HEADS):
            _route_head(h, True, q_ref, keys_ref, *out_refs)


def _route(q, keys_stack, layer):
    t = q.shape[0]
    tr = TR_ROUTE
    shp1 = (PEER_NKEYS, PEER_HEADS, t)
    spec1 = pl.BlockSpec((PEER_NKEYS, PEER_HEADS, tr), lambda i: (0, 0, i))
    shp2 = (PEER_HEADS, PEER_NKEYS // 2, t)
    spec2 = pl.BlockSpec((PEER_HEADS, PEER_NKEYS // 2, tr), lambda i: (0, 0, i))
    return pl.pallas_call(
        _route_kernel,
        out_shape=(jax.ShapeDtypeStruct(shp1, I32), jax.ShapeDtypeStruct(shp1, I32),
                   jax.ShapeDtypeStruct(shp2, I32), jax.ShapeDtypeStruct(shp2, I32)),
        grid=(t // tr,),
        in_specs=[pl.BlockSpec((tr, q.shape[1]), lambda i: (i, 0)),
                  pl.BlockSpec((None,) + keys_stack.shape[1:], lambda i: (layer, 0, 0, 0, 0))],
        out_specs=(spec1, spec1, spec2, spec2),
        compiler_params=_cparams(("parallel",)), name="peer_route",
    )(q, keys_stack)


def _gelu_gate(x, g):
    c = 2.0 * math.sqrt(2.0 / math.pi) * math.log2(math.e)
    e = jnp.exp2(x * (-c - (c * 0.044715) * (x * x)))
    return (x * g) / (1.0 + e)


def _bf16_rows(word_row):
    words = jnp.broadcast_to(word_row, (PEER_NKEYS // 2, word_row.shape[1]))
    return pltpu.bitcast(words, BF16)


def _peer_dense_kernel(n_steps, h_ref, len_ref, e1_ref, r2_ref, e2_ref, u_ref, v_ref, y_ref, s0, s1, ga):
    step = pl.program_id(0)
    nt = (((1,), (1,)), ((), ()))
    tm = h_ref.shape[0]
    d = v_ref.shape[1]
    half_e = E_STEP // 2

    @pl.when(step == 0)
    def _():
        s0[...] = jnp.zeros_like(s0)
        s1[...] = jnp.zeros_like(s1)

    @pl.when((step == 0) | (lax.rem(step + n_steps - 1, n_steps) == 0))
    def _():
        y_ref[...] = jnp.zeros_like(y_ref)

    def body(s_new, s_old):
        def scores(half):
            rows = slice(half * half_e, (half + 1) * half_e)
            s_new[:, rows] = lax.dot_general(h_ref[...], u_ref[rows, :].astype(BF16), nt,
                                             preferred_element_type=F32)

        def gates(r, c):
            tok = slice(c * LANES, (c + 1) * LANES)
            gt = None
            for hd in range(PEER_HEADS):
                r2 = pltpu.bitcast(r2_ref[hd, :, tok], BF16)
                e2 = pltpu.bitcast(e2_ref[hd, :, tok], BF16)
                hit = r2 < _bf16_rows(len_ref[r, hd:hd + 1, tok])
                term = jnp.where(hit, e2, 0.0) * _bf16_rows(e1_ref[r, hd:hd + 1, tok])
                gt = term if gt is None else gt + term
            cols = slice(r * PEER_NKEYS, (r + 1) * PEER_NKEYS)
            ga[tok, cols] = _gelu_gate(s_old[tok, cols], gt.astype(F32).T).astype(BF16)

        def contract(piece, n_pieces):
            w = d // n_pieces
            cols = slice(piece * w, (piece + 1) * w)
            y_ref[:, cols] += jnp.dot(ga[...], v_ref[:, cols].astype(BF16), preferred_element_type=F32)

        scores(0)
        scores(1)
        for r in range(A_STEP):
            for c in range(tm // LANES):
                gates(r, c)
        for piece in range(4):
            contract(piece, 4)

    even = lax.rem(step, 2) == 0

    @pl.when(even)
    def _():
        body(s0, s1)

    @pl.when(jnp.logical_not(even))
    def _():
        body(s1, s0)


def _peer_dense(h, route, u_stack, v_stack, layer):
    t, d = h.shape
    e = u_stack.shape[1]
    tm = min(TM_PEER, t)
    n_steps = e // E_STEP
    len_a, e1, r2, e2 = route
    once = pl.Buffered(1)
    n_pairs = (t // tm) * n_steps

    def pair(g, lag):
        p = jnp.clip(g - lag, 0, n_pairs - 1)
        return p // n_steps, p % n_steps

    aspec = pl.BlockSpec((A_STEP, PEER_HEADS, tm), lambda g: (pair(g, 1)[1], 0, pair(g, 1)[0]))
    bspec = pl.BlockSpec((PEER_HEADS, PEER_NKEYS // 2, tm), lambda g: (0, 0, pair(g, 1)[0]), pipeline_mode=once)
    return pl.pallas_call(
        functools.partial(_peer_dense_kernel, n_steps),
        out_shape=jax.ShapeDtypeStruct((t, d), F32),
        grid=(n_pairs + 1,),
        in_specs=[pl.BlockSpec((tm, d), lambda g: (pair(g, 0)[0], 0), pipeline_mode=once),
                  aspec, aspec, bspec, bspec,
                  pl.BlockSpec((None, E_STEP, d), lambda g: (layer, pair(g, 0)[1], 0)),
                  pl.BlockSpec((None, E_STEP, d), lambda g: (layer, pair(g, 1)[1], 0))],
        out_specs=pl.BlockSpec((tm, d), lambda g: (pair(g, 1)[0], 0)),
        scratch_shapes=[pltpu.VMEM((tm, E_STEP), F32), pltpu.VMEM((tm, E_STEP), F32),
                        pltpu.VMEM((tm, E_STEP), BF16)],
        compiler_params=_cparams(("arbitrary",)), name="peer_dense",
    )(h, len_a, e1, r2, e2, u_stack, v_stack)


def kernel(x_prompt, x_sample, cache_k, cache_v, c, c_ctx, norm_g, final_g, w_mod, b_mod, attn_w_qkv, attn_q_g, attn_k_g, attn_w_o, sgu_w_in, sgu_g, sgu_w_s, sgu_b, sgu_w_out, conv_w_in, conv_w, conv_w_out, peer_w_q, peer_keys, peer_u, peer_v):
    nb, sl, d = x_prompt.shape
    db, dsl, _ = x_sample.shape
    depth = w_mod.shape[0]
    n_ctx = nb * sl
    assert d == D_MODEL and dsl == SEQ_ALIGN and n_ctx == N_CTX_TOKENS and SEQ_ALIGN % sl == 0
    assert 1 + db <= MOD_ROWS

    x = jnp.concatenate([x_prompt.reshape(n_ctx, d), x_sample.reshape(db * dsl, d)], axis=0)
    cvec = jnp.concatenate([c_ctx[None, :], c, jnp.zeros((MOD_ROWS - 1 - db, d), F32)], axis=0)
    mods = _modulation(cvec, w_mod, b_mod).reshape(depth, MOD_ROWS, N_MOD, 1, d)

    cos_f, sin_s = _rope_tables(dsl)
    nq, nk = N_HEADS * HEAD_DIM, N_KV_HEADS * HEAD_DIM
    new_k, new_v = [], []
    y = None
    for i in range(depth):
        kind, j = i % N_MIXERS, i // N_MIXERS
        w_in, act = ((attn_w_qkv, None), (sgu_w_in, "gelu"), (conv_w_in, None))[kind]
        x, _, mixed = _norm_matmul(x, y, mods, (i - 1, 5), (i, 0), (i, 1), norm_g[i, 0], w_in, j, act=act)
        if kind == 0:
            qkv = mixed
            g_full = jnp.concatenate([jnp.tile(attn_q_g[j], N_HEADS), jnp.tile(attn_k_g[j], N_KV_HEADS)])
            qn, kn = _qk_post(qkv, g_full.reshape(1, nq + nk), cos_f, sin_s, n_ctx)
            new_k.append(kn[:n_ctx].reshape(nb, sl, N_KV_HEADS, HEAD_DIM))
            new_v.append(qkv[:n_ctx, nq + nk:].reshape(nb, sl, N_KV_HEADS, HEAD_DIM))
            o_ctx = _attention(qn, kn, qkv, 0, nb, sl)
            ck = jnp.transpose(cache_k[:, j], (0, 2, 1, 3))
            cv = jnp.transpose(cache_v[:, j], (0, 2, 1, 3))
            o_lat = _attention(qn, kn, qkv, n_ctx, db, dsl, ck, cv)
            o = _matmul(jnp.concatenate([o_ctx, o_lat], axis=0), attn_w_o, j)
        elif kind == 1:
            b_full = jnp.repeat(sgu_b[j].T, SGU_CHUNK, axis=1)
            o = _matmul(_sgu_mid(mixed, sgu_g[j], sgu_w_s[j], b_full), sgu_w_out, j)
        else:
            cw = jnp.concatenate([conv_w[j], jnp.zeros((8 - conv_w.shape[1], d), F32)], axis=0)
            o = _matmul(_conv_mid(mixed, cw, n_ctx, sl, dsl), conv_w_out, j)
        x, h, q = _norm_matmul(x, o, mods, (i, 2), (i, 3), (i, 4), norm_g[i, 1], peer_w_q, i,
                               want_h=True, out_dtype=BF16)
        y = _peer_dense(h, _route(q, peer_keys, i), peer_u, peer_v, i)

    out = _final_norm(x, y, mods, (depth - 1, 5), final_g)
    y_prompt = out[:n_ctx].reshape(nb, sl, d)
    y_sample = out[n_ctx:].reshape(db, dsl, d)
    return (y_prompt, y_sample, jnp.stack(new_k, axis=1), jnp.stack(new_v, axis=1))
```

```python
import functools
import math

import jax
import jax.numpy as jnp
from jax import lax
from jax.experimental import pallas as pl
from jax.experimental.pallas import tpu as pltpu

F32 = jnp.float32
BF16 = jnp.bfloat16
I32 = jnp.int32

D_MODEL = 2048
HEAD_DIM = 128
N_HEADS = 16
N_KV_HEADS = 4
KV_GROUP = N_HEADS // N_KV_HEADS
GRID_W = 64
ROPE_THETA = 10000.0
AXIS_PAIRS = HEAD_DIM // 4
SGU_CHUNK = 128
SGU_GROUPS = 16
PEER_HEADS = 8
PEER_NKEYS = 128
PEER_TOPK = 16
PEER_HALF = 128
N_MOD = 6
EPS = 1e-6
N_MIXERS = 3
N_CTX_TOKENS = 16 * 256

LANES = 128
VMEM_LIMIT = 56 * 1024 * 1024

SEQ_ALIGN = 1024
MOD_ROWS = 8
TM = 1024
TN = 512
TR_NM = 256
TR_ROUTE = 256
TM_PEER = 1024
A_STEP = 4
E_STEP = A_STEP * PEER_NKEYS


def _cparams(sem):
    return pltpu.CompilerParams(dimension_semantics=sem, vmem_limit_bytes=VMEM_LIMIT)


def _mod_row(tile_idx, tile_rows):
    per = SEQ_ALIGN // tile_rows
    return jnp.maximum(tile_idx // per - (N_CTX_TOKENS // SEQ_ALIGN - 1), 0)


def _gelu(x):
    c = math.sqrt(2.0 / math.pi)
    return 0.5 * x * (1.0 + jnp.tanh(c * (x + 0.044715 * (x * x * x))))


def _mod_kernel(c_ref, w_ref, b_ref, o_ref):
    c = c_ref[...]
    s = c * (1.0 / (1.0 + jnp.exp(-c)))
    o_ref[...] = jnp.dot(s, w_ref[...], preferred_element_type=F32) + b_ref[...]


def _modulation(cvec, w_mod, b_mod):
    depth, d, n = w_mod.shape
    tn = 1024
    return pl.pallas_call(
        _mod_kernel,
        out_shape=jax.ShapeDtypeStruct((depth, MOD_ROWS, n), F32),
        grid=(depth, n // tn),
        in_specs=[
            pl.BlockSpec((MOD_ROWS, d), lambda l, j: (0, 0)),
            pl.BlockSpec((None, d, tn), lambda l, j: (l, 0, j)),
            pl.BlockSpec((None, 1, tn), lambda l, j: (l, 0, j)),
        ],
        out_specs=pl.BlockSpec((None, MOD_ROWS, tn), lambda l, j: (l, 0, j)),
        compiler_params=_cparams(("parallel", "parallel")),
        name="modulation",
    )(cvec, w_mod, b_mod.reshape(depth, 1, n))


def _norm_mod_kernel(has_res, x_ref, *refs):
    if has_res:
        y_ref, gate_ref, g_ref, shift_ref, scale_ref, xo_ref, h_ref = refs
        x = x_ref[...] + gate_ref[...] * y_ref[...]
        xo_ref[...] = x
    else:
        g_ref, shift_ref, scale_ref, h_ref = refs
        x = x_ref[...]
    ms = jnp.mean(x * x, axis=-1, keepdims=True)
    y = x * lax.rsqrt(ms + EPS)
    y = y * g_ref[...]
    h_ref[...] = (y * (1.0 + scale_ref[...]) + shift_ref[...]).astype(h_ref.dtype)


def _norm_mod(x, y, mods, gate_ln, shift_ln, scale_ln, g):
    t, d = x.shape
    tr = TR_NM
    row_spec = pl.BlockSpec((tr, d), lambda i: (i, 0))

    def mod_spec(ln):
        return pl.BlockSpec((None, None, None, 1, d),
                            lambda i: (ln[0], _mod_row(i, tr), ln[1], 0, 0))

    g_spec = pl.BlockSpec((1, d), lambda i: (0, 0))
    has_res = y is not None
    if has_res:
        args = (x, y, mods, g.reshape(1, d), mods, mods)
        in_specs = [row_spec, row_spec, mod_spec(gate_ln), g_spec, mod_spec(shift_ln), mod_spec(scale_ln)]
        out_shape = (jax.ShapeDtypeStruct((t, d), F32), jax.ShapeDtypeStruct((t, d), BF16))
        out_specs = (row_spec, row_spec)
    else:
        args = (x, g.reshape(1, d), mods, mods)
        in_specs = [row_spec, g_spec, mod_spec(shift_ln), mod_spec(scale_ln)]
        out_shape = jax.ShapeDtypeStruct((t, d), BF16)
        out_specs = row_spec
    out = pl.pallas_call(
        functools.partial(_norm_mod_kernel, has_res),
        out_shape=out_shape, grid=(t // tr,), in_specs=in_specs, out_specs=out_specs,
        compiler_params=_cparams(("parallel",)), name="norm_mod",
    )(*args)
    return out if has_res else (x, out)


def _final_norm_kernel(x_ref, y_ref, gate_ref, g_ref, o_ref):
    x = x_ref[...] + gate_ref[...] * y_ref[...]
    ms = jnp.mean(x * x, axis=-1, keepdims=True)
    o_ref[...] = x * lax.rsqrt(ms + EPS) * g_ref[...]


def _final_norm(x, y, mods, gate_ln, g):
    t, d = x.shape
    tr = TR_NM
    row_spec = pl.BlockSpec((tr, d), lambda i: (i, 0))
    return pl.pallas_call(
        _final_norm_kernel,
        out_shape=jax.ShapeDtypeStruct((t, d), F32), grid=(t // tr,),
        in_specs=[row_spec, row_spec,
                  pl.BlockSpec((None, None, None, 1, d),
                               lambda i: (gate_ln[0], _mod_row(i, tr), gate_ln[1], 0, 0)),
                  pl.BlockSpec((1, d), lambda i: (0, 0))],
        out_specs=row_spec,
        compiler_params=_cparams(("parallel",)), name="final_norm",
    )(x, y, mods, g.reshape(1, d))


def _matmul_kernel(act, a_ref, w_ref, o_ref):
    acc = jnp.dot(a_ref[...], w_ref[...].astype(BF16), preferred_element_type=F32)
    if act == "gelu":
        acc = _gelu(acc)
    o_ref[...] = acc.astype(o_ref.dtype)


def _matmul(a, w_stack, layer, out_dtype=F32, act=None):
    m, k = a.shape
    n = w_stack.shape[2]
    tm = min(TM, m)
    return pl.pallas_call(
        functools.partial(_matmul_kernel, act),
        out_shape=jax.ShapeDtypeStruct((m, n), out_dtype),
        grid=(m // tm, n // TN),
        in_specs=[pl.BlockSpec((tm, k), lambda i, j: (i, 0)),
                  pl.BlockSpec((None, k, TN), lambda i, j: (layer, 0, j))],
        out_specs=pl.BlockSpec((tm, TN), lambda i, j: (i, j)),
        compiler_params=_cparams(("parallel", "arbitrary")), name="matmul",
    )(a, w_stack)


def _rope_tables(seq_len):
    rows = seq_len // GRID_W
    t = jnp.arange(seq_len)
    r = jnp.repeat(jnp.arange(rows), GRID_W).astype(F32)
    col = (t % GRID_W).astype(F32)
    inv = ROPE_THETA ** (-jnp.arange(AXIS_PAIRS, dtype=F32) / AXIS_PAIRS)
    ang = jnp.concatenate([r[:, None] * inv, col[:, None] * inv], axis=-1)
    cos, sin = jnp.cos(ang), jnp.sin(ang)
    cos_f = jnp.repeat(cos, 2, axis=-1)
    sin_s = jnp.stack([-sin, sin], axis=-1).reshape(seq_len, HEAD_DIM)
    return cos_f, sin_s


def _qk_post_kernel(n_ctx_tiles, qkv_ref, g_ref, cos_ref, sin_ref, q_ref, k_ref):
    is_lat = pl.program_id(0) >= n_ctx_tiles

    def heads(with_rope):
        if with_rope:
            cos, sin = cos_ref[...], sin_ref[...]
            even = (lax.broadcasted_iota(I32, cos.shape, 1) & 1) == 0
        for hd in range(N_HEADS + N_KV_HEADS):
            x = qkv_ref[:, hd * HEAD_DIM:(hd + 1) * HEAD_DIM]
            ms = jnp.mean(x * x, axis=-1, keepdims=True)
            y = x * lax.rsqrt(ms + EPS) * g_ref[:, hd * HEAD_DIM:(hd + 1) * HEAD_DIM]
            if with_rope:
                partner = jnp.where(even, pltpu.roll(y, HEAD_DIM - 1, 1), pltpu.roll(y, 1, 1))
                y = y * cos + partner * sin
            if hd < N_HEADS:
                q_ref[:, hd * HEAD_DIM:(hd + 1) * HEAD_DIM] = y.astype(q_ref.dtype)
            else:
                kh = hd - N_HEADS
                k_ref[:, kh * HEAD_DIM:(kh + 1) * HEAD_DIM] = y

    @pl.when(is_lat)
    def _():
        heads(True)

    @pl.when(jnp.logical_not(is_lat))
    def _():
        heads(False)


def _qk_post(qkv, g_full, cos_f, sin_s, n_ctx_tokens):
    t = qkv.shape[0]
    tr = 256
    nq, nk = N_HEADS * HEAD_DIM, N_KV_HEADS * HEAD_DIM
    lat_tiles = cos_f.shape[0] // tr
    n_ctx_tiles = n_ctx_tokens // tr

    def pos_map(i):
        return (jnp.maximum(i - n_ctx_tiles, 0) % lat_tiles, 0)

    return pl.pallas_call(
        functools.partial(_qk_post_kernel, n_ctx_tiles),
        out_shape=(jax.ShapeDtypeStruct((t, nq), BF16), jax.ShapeDtypeStruct((t, nk), F32)),
        grid=(t // tr,),
        in_specs=[pl.BlockSpec((tr, nq + nk), lambda i: (i, 0)),
                  pl.BlockSpec((1, nq + nk), lambda i: (0, 0)),
                  pl.BlockSpec((tr, HEAD_DIM), pos_map),
                  pl.BlockSpec((tr, HEAD_DIM), pos_map)],
        out_specs=(pl.BlockSpec((tr, nq), lambda i: (i, 0)), pl.BlockSpec((tr, nk), lambda i: (i, 0))),
        compiler_params=_cparams(("parallel",)), name="qk_post",
    )(qkv, g_full, cos_f, sin_s)


def _attn_kernel(has_cache, q_ref, k_ref, v_ref, *refs):
    if has_cache:
        ck_ref, cv_ref, o_ref = refs
    else:
        (o_ref,) = refs
    scale = HEAD_DIM ** -0.5
    k = k_ref[...].astype(BF16)
    v = v_ref[...].astype(BF16)
    if has_cache:
        ck = ck_ref[...].astype(BF16)
        cv = cv_ref[...].astype(BF16)
    nt = (((1,), (1,)), ((), ()))
    for g in range(KV_GROUP):
        q = q_ref[:, g * HEAD_DIM:(g + 1) * HEAD_DIM]
        s1 = lax.dot_general(q, k, nt, preferred_element_type=F32) * scale
        m = jnp.max(s1, axis=-1, keepdims=True)
        if has_cache:
            s2 = lax.dot_general(q, ck, nt, preferred_element_type=F32) * scale
            m = jnp.maximum(m, jnp.max(s2, axis=-1, keepdims=True))
        p1 = jnp.exp(s1 - m)
        l = jnp.sum(p1, axis=-1, keepdims=True)
        if has_cache:
            p2 = jnp.exp(s2 - m)
            l = l + jnp.sum(p2, axis=-1, keepdims=True)
        inv = 1.0 / l
        o = jnp.dot((p1 * inv).astype(BF16), v, preferred_element_type=F32)
        if has_cache:
            o = o + jnp.dot((p2 * inv).astype(BF16), cv, preferred_element_type=F32)
        o_ref[:, g * HEAD_DIM:(g + 1) * HEAD_DIM] = o.astype(o_ref.dtype)


def _attention(qn, kn, qkv, row0, n_seq, seq_len, cache_k=None, cache_v=None):
    tq = 256
    qb = seq_len // tq
    r0q = row0 // tq
    r0k = row0 // seq_len
    gw = KV_GROUP * HEAD_DIM
    v_col0 = (N_HEADS + N_KV_HEADS)
    has_cache = cache_k is not None
    in_specs = [
        pl.BlockSpec((tq, gw), lambda b, h, i: (r0q + b * qb + i, h)),
        pl.BlockSpec((seq_len, HEAD_DIM), lambda b, h, i: (r0k + b, h)),
        pl.BlockSpec((seq_len, HEAD_DIM), lambda b, h, i: (r0k + b, v_col0 + h)),
    ]
    args = [qn, kn, qkv]
    if has_cache:
        past = cache_k.shape[2]
        cspec = pl.BlockSpec((None, None, past, HEAD_DIM), lambda b, h, i: (b, h, 0, 0))
        in_specs += [cspec, cspec]
        args += [cache_k, cache_v]
    return pl.pallas_call(
        functools.partial(_attn_kernel, has_cache),
        out_shape=jax.ShapeDtypeStruct((n_seq * seq_len, N_HEADS * HEAD_DIM), BF16),
        grid=(n_seq, N_KV_HEADS, qb),
        in_specs=in_specs,
        out_specs=pl.BlockSpec((tq, gw), lambda b, h, i: (b * qb + i, h)),
        compiler_params=_cparams(("parallel", "parallel", "arbitrary")), name="attention",
    )(*args)


def _sgu_mid_kernel(u_ref, v_ref, g_ref, ws_ref, b_ref, o_ref):
    v = v_ref[...]
    ms = jnp.mean(v * v, axis=-1, keepdims=True)
    vn = (v * lax.rsqrt(ms + EPS) * g_ref[...]).astype(BF16)
    for g in range(SGU_GROUPS):
        sl = slice(g * LANES, (g + 1) * LANES)
        mixed = jnp.dot(ws_ref[g].astype(BF16), vn[:, sl], preferred_element_type=F32) + b_ref[:, sl]
        o_ref[:, sl] = (u_ref[:, sl] * mixed).astype(o_ref.dtype)


def _sgu_mid(z, g_v, w_s, b_full):
    t = z.shape[0]
    w = z.shape[1] // 2
    ch = SGU_CHUNK
    return pl.pallas_call(
        _sgu_mid_kernel,
        out_shape=jax.ShapeDtypeStruct((t, w), BF16),
        grid=(t // ch,),
        in_specs=[pl.BlockSpec((ch, w), lambda i: (i, 0)),
                  pl.BlockSpec((ch, w), lambda i: (i, 1)),
                  pl.BlockSpec((1, w), lambda i: (0, 0)),
                  pl.BlockSpec((SGU_GROUPS, ch, ch), lambda i: (0, 0, 0)),
                  pl.BlockSpec((ch, w), lambda i: (0, 0))],
        out_specs=pl.BlockSpec((ch, w), lambda i: (i, 0)),
        compiler_params=_cparams(("parallel",)), name="sgu_mid",
    )(z, z, g_v.reshape(1, w), w_s, b_full)


def _conv_mid_kernel(n_ctx_tiles, ctx_len, lat_len, bg_ref, cg_ref, z_ref, w_ref, o_ref):
    z = cg_ref[...] * z_ref[...]
    rows = z.shape[0]
    seq_len = jnp.where(pl.program_id(0) < n_ctx_tiles, ctx_len, lat_len)
    pos = lax.broadcasted_iota(I32, z.shape, 0) & (seq_len - 1)
    prev = jnp.where(pos == 0, 0.0, pltpu.roll(z, 1, 0))
    nxt = jnp.where(pos == seq_len - 1, 0.0, pltpu.roll(z, rows - 1, 0))
    zc = w_ref[0:1, :] * prev + w_ref[1:2, :] * z + w_ref[2:3, :] * nxt
    o_ref[...] = (bg_ref[...] * zc).astype(o_ref.dtype)


def _conv_mid(y, conv_w, n_ctx_tokens, ctx_len, lat_len):
    t = y.shape[0]
    d = y.shape[1] // 3
    tr, tc = SEQ_ALIGN, 512
    nc = d // tc
    return pl.pallas_call(
        functools.partial(_conv_mid_kernel, n_ctx_tokens // tr, ctx_len, lat_len),
        out_shape=jax.ShapeDtypeStruct((t, d), BF16),
        grid=(t // tr, nc),
        in_specs=[pl.BlockSpec((tr, tc), lambda i, j: (i, j)),
                  pl.BlockSpec((tr, tc), lambda i, j: (i, nc + j)),
                  pl.BlockSpec((tr, tc), lambda i, j: (i, 2 * nc + j)),
                  pl.BlockSpec((8, tc), lambda i, j: (0, j))],
        out_specs=pl.BlockSpec((tr, tc), lambda i, j: (i, j)),
        compiler_params=_cparams(("parallel", "parallel")), name="conv_mid",
    )(y, y, y, conv_w)


_CAND = [(i, j) for i in range(PEER_TOPK) for j in range(PEER_TOPK) if (i + 1) * (j + 1) <= PEER_TOPK]


def _top_list(s, k, exact_ties):
    n = s.shape[0]
    if exact_ties:
        iota = lax.broadcasted_iota(I32, s.shape, 0)
    vals = []
    for r in range(k):
        m = jnp.max(s, axis=0, keepdims=True)
        hit = s == m
        if exact_ties:
            idx = jnp.min(jnp.where(hit, iota, n), axis=0, keepdims=True)
            hit = iota == idx
        s = jnp.where(hit, -_TAKEN * (1.0 + r / k), s)
        vals.append(m)
    rank = jnp.where(s < -0.5 * _TAKEN, (s * (-1.0 / _TAKEN) - 1.0) * k, float(k))
    return vals, rank


_TAKEN = 2.0 ** 126


def _ranked_count_off(rank, k):
    return jnp.sum(jnp.where(rank < k, 1, 0), axis=0, keepdims=True) != k


def _bf16_pair_word(x):
    hi = pltpu.bitcast(x.astype(BF16).astype(F32), I32)
    return hi | lax.shift_right_logical(hi, 16)


def _route_head(h, exact_halves, q_ref, keys_ref, len_ref, e1_ref, r2_ref, e2_ref):
    nt = (((1,), (1,)), ((), ()))
    k = PEER_TOPK
    sc = []
    for c in range(2):
        col = (2 * h + c) * PEER_HALF
        sc.append(lax.dot_general(keys_ref[h, c].astype(BF16), q_ref[:, col:col + PEER_HALF], nt,
                                  preferred_element_type=F32))
    v1, rank1 = _top_list(sc[0], k, exact_halves)
    v2, rank2 = _top_list(sc[1], k, exact_halves)
    cand = jnp.concatenate([v1[i] + v2[j] for (i, j) in _CAND], axis=0)
    _, crank = _top_list(cand, k, True)
    sel = crank < k
    ev1 = [jnp.exp(v - v1[0]) for v in v1]
    ev2 = [jnp.exp(v - v2[0]) for v in v2]
    zsum = jnp.zeros_like(v1[0])
    row_len = [jnp.zeros(v1[0].shape, F32) for _ in range(k)]
    for r, (i, j) in enumerate(_CAND):
        s_r = sel[r:r + 1, :]
        zsum = zsum + jnp.where(s_r, ev1[i] * ev2[j], 0.0)
        row_len[i] = row_len[i] + jnp.where(s_r, 1.0, 0.0)
    len_a = jnp.zeros(rank1.shape, F32)
    for i in range(k):
        len_a = jnp.where(rank1 == i, row_len[i], len_a)
    len_ref[:, h, :] = _bf16_pair_word(len_a)
    e1_ref[:, h, :] = _bf16_pair_word(jnp.exp(sc[0] - v1[0]))
    r2_ref[h] = pltpu.bitcast(rank2.astype(BF16), I32)
    e2_ref[h] = pltpu.bitcast((jnp.exp(sc[1] - v2[0]) / zsum).astype(BF16), I32)
    if exact_halves:
        return None
    return _ranked_count_off(rank1, k) | _ranked_count_off(rank2, k)


def _route_kernel(q_ref, keys_ref, *out_refs):
    tie = None
    for h in range(PEER_HEADS):
        t = _route_head(h, False, q_ref, keys_ref, *out_refs)
        tie = t if tie is None else tie | t

    @pl.when(jnp.max(jnp.where(tie, 1, 0)) > 0)
    def _():
        for h in range(PEER_HEADS):
            _route_head(h, True, q_ref, keys_ref, *out_refs)


def _route(q, keys_stack, layer):
    t = q.shape[0]
    tr = TR_ROUTE
    shp1 = (PEER_NKEYS, PEER_HEADS, t)
    spec1 = pl.BlockSpec((PEER_NKEYS, PEER_HEADS, tr), lambda i: (0, 0, i))
    shp2 = (PEER_HEADS, PEER_NKEYS // 2, t)
    spec2 = pl.BlockSpec((PEER_HEADS, PEER_NKEYS // 2, tr), lambda i: (0, 0, i))
    return pl.pallas_call(
        _route_kernel,
        out_shape=(jax.ShapeDtypeStruct(shp1, I32), jax.ShapeDtypeStruct(shp1, I32),
                   jax.ShapeDtypeStruct(shp2, I32), jax.ShapeDtypeStruct(shp2, I32)),
        grid=(t // tr,),
        in_specs=[pl.BlockSpec((tr, q.shape[1]), lambda i: (i, 0)),
                  pl.BlockSpec((None,) + keys_stack.shape[1:], lambda i: (layer, 0, 0, 0, 0))],
        out_specs=(spec1, spec1, spec2, spec2),
        compiler_params=_cparams(("parallel",)), name="peer_route",
    )(q, keys_stack)


def _gelu_gate(x, g):
    c = 2.0 * math.sqrt(2.0 / math.pi) * math.log2(math.e)
    e = jnp.exp2(x * (-c - (c * 0.044715) * (x * x)))
    return (x * g) / (1.0 + e)


def _bf16_rows(word_row):
    words = jnp.broadcast_to(word_row, (PEER_NKEYS // 2, word_row.shape[1]))
    return pltpu.bitcast(words, BF16)


def _peer_dense_kernel(n_steps, h_ref, len_ref, e1_ref, r2_ref, e2_ref, u_ref, v_ref, y_ref, s0, s1, ga):
    step = pl.program_id(0)
    nt = (((1,), (1,)), ((), ()))
    tm = h_ref.shape[0]
    d = v_ref.shape[1]
    half_e = E_STEP // 2

    @pl.when(step == 0)
    def _():
        s0[...] = jnp.zeros_like(s0)
        s1[...] = jnp.zeros_like(s1)

    @pl.when((step == 0) | (lax.rem(step + n_steps - 1, n_steps) == 0))
    def _():
        y_ref[...] = jnp.zeros_like(y_ref)

    def body(s_new, s_old):
        def scores(half):
            rows = slice(half * half_e, (half + 1) * half_e)
            s_new[:, rows] = lax.dot_general(h_ref[...], u_ref[rows, :].astype(BF16), nt,
                                             preferred_element_type=F32)

        def gates(r, c):
            tok = slice(c * LANES, (c + 1) * LANES)
            gt = None
            for hd in range(PEER_HEADS):
                r2 = pltpu.bitcast(r2_ref[hd, :, tok], BF16)
                e2 = pltpu.bitcast(e2_ref[hd, :, tok], BF16)
                hit = r2 < _bf16_rows(len_ref[r, hd:hd + 1, tok])
                term = jnp.where(hit, e2, 0.0) * _bf16_rows(e1_ref[r, hd:hd + 1, tok])
                gt = term if gt is None else gt + term
            cols = slice(r * PEER_NKEYS, (r + 1) * PEER_NKEYS)
            ga[tok, cols] = _gelu_gate(s_old[tok, cols].astype(BF16), gt.T)

        def contract(piece, n_pieces):
            w = d // n_pieces
            cols = slice(piece * w, (piece + 1) * w)
            y_ref[:, cols] += jnp.dot(ga[...], v_ref[:, cols].astype(BF16), preferred_element_type=F32)

        scores(0)
        scores(1)
        for r in range(A_STEP):
            for c in range(tm // LANES):
                gates(r, c)
        for piece in range(4):
            contract(piece, 4)

    even = lax.rem(step, 2) == 0

    @pl.when(even)
    def _():
        body(s0, s1)

    @pl.when(jnp.logical_not(even))
    def _():
        body(s1, s0)


def _peer_dense(h, route, u_stack, v_stack, layer):
    t, d = h.shape
    e = u_stack.shape[1]
    tm = min(TM_PEER, t)
    n_steps = e // E_STEP
    len_a, e1, r2, e2 = route
    once = pl.Buffered(1)
    n_pairs = (t // tm) * n_steps

    def pair(g, lag):
        p = jnp.clip(g - lag, 0, n_pairs - 1)
        return p // n_steps, p % n_steps

    aspec = pl.BlockSpec((A_STEP, PEER_HEADS, tm), lambda g: (pair(g, 1)[1], 0, pair(g, 1)[0]))
    bspec = pl.BlockSpec((PEER_HEADS, PEER_NKEYS // 2, tm), lambda g: (0, 0, pair(g, 1)[0]), pipeline_mode=once)
    return pl.pallas_call(
        functools.partial(_peer_dense_kernel, n_steps),
        out_shape=jax.ShapeDtypeStruct((t, d), F32),
        grid=(n_pairs + 1,),
        in_specs=[pl.BlockSpec((tm, d), lambda g: (pair(g, 0)[0], 0), pipeline_mode=once),
                  aspec, aspec, bspec, bspec,
                  pl.BlockSpec((None, E_STEP, d), lambda g: (layer, pair(g, 0)[1], 0)),
                  pl.BlockSpec((None, E_STEP, d), lambda g: (layer, pair(g, 1)[1], 0))],
        out_specs=pl.BlockSpec((tm, d), lambda g: (pair(g, 1)[0], 0)),
        scratch_shapes=[pltpu.VMEM((tm, E_STEP), F32), pltpu.VMEM((tm, E_STEP), F32),
                        pltpu.VMEM((tm, E_STEP), BF16)],
        compiler_params=_cparams(("arbitrary",)), name="peer_dense",
    )(h, len_a, e1, r2, e2, u_stack, v_stack)


def kernel(x_prompt, x_sample, cache_k, cache_v, c, c_ctx, norm_g, final_g, w_mod, b_mod, attn_w_qkv, attn_q_g, attn_k_g, attn_w_o, sgu_w_in, sgu_g, sgu_w_s, sgu_b, sgu_w_out, conv_w_in, conv_w, conv_w_out, peer_w_q, peer_keys, peer_u, peer_v):
    nb, sl, d = x_prompt.shape
    db, dsl, _ = x_sample.shape
    depth = w_mod.shape[0]
    n_ctx = nb * sl
    assert d == D_MODEL and dsl == SEQ_ALIGN and n_ctx == N_CTX_TOKENS and SEQ_ALIGN % sl == 0
    assert 1 + db <= MOD_ROWS

    x = jnp.concatenate([x_prompt.reshape(n_ctx, d), x_sample.reshape(db * dsl, d)], axis=0)
    cvec = jnp.concatenate([c_ctx[None, :], c, jnp.zeros((MOD_ROWS - 1 - db, d), F32)], axis=0)
    mods = _modulation(cvec, w_mod, b_mod).reshape(depth, MOD_ROWS, N_MOD, 1, d)

    cos_f, sin_s = _rope_tables(dsl)
    nq, nk = N_HEADS * HEAD_DIM, N_KV_HEADS * HEAD_DIM
    new_k, new_v = [], []
    y = None
    for i in range(depth):
        kind, j = i % N_MIXERS, i // N_MIXERS
        x, h = _norm_mod(x, y, mods, (i - 1, 5), (i, 0), (i, 1), norm_g[i, 0])
        if kind == 0:
            qkv = _matmul(h, attn_w_qkv, j)
            g_full = jnp.concatenate([jnp.tile(attn_q_g[j], N_HEADS), jnp.tile(attn_k_g[j], N_KV_HEADS)])
            qn, kn = _qk_post(qkv, g_full.reshape(1, nq + nk), cos_f, sin_s, n_ctx)
            new_k.append(kn[:n_ctx].reshape(nb, sl, N_KV_HEADS, HEAD_DIM))
            new_v.append(qkv[:n_ctx, nq + nk:].reshape(nb, sl, N_KV_HEADS, HEAD_DIM))
            o_ctx = _attention(qn, kn, qkv, 0, nb, sl)
            ck = jnp.transpose(cache_k[:, j], (0, 2, 1, 3))
            cv = jnp.transpose(cache_v[:, j], (0, 2, 1, 3))
            o_lat = _attention(qn, kn, qkv, n_ctx, db, dsl, ck, cv)
            o = _matmul(jnp.concatenate([o_ctx, o_lat], axis=0), attn_w_o, j)
        elif kind == 1:
            z = _matmul(h, sgu_w_in, j, act="gelu")
            b_full = jnp.repeat(sgu_b[j].T, SGU_CHUNK, axis=1)
            o = _matmul(_sgu_mid(z, sgu_g[j], sgu_w_s[j], b_full), sgu_w_out, j)
        else:
            yc = _matmul(h, conv_w_in, j)
            cw = jnp.concatenate([conv_w[j], jnp.zeros((8 - conv_w.shape[1], d), F32)], axis=0)
            o = _matmul(_conv_mid(yc, cw, n_ctx, sl, dsl), conv_w_out, j)
        x, h = _norm_mod(x, o, mods, (i, 2), (i, 3), (i, 4), norm_g[i, 1])
        q = _matmul(h, peer_w_q, i, out_dtype=BF16)
        y = _peer_dense(h, _route(q, peer_keys, i), peer_u, peer_v, i)

    out = _final_norm(x, y, mods, (depth - 1, 5), final_g)
    y_prompt = out[:n_ctx].reshape(nb, sl, d)
    y_sample = out[n_ctx:].reshape(db, dsl, d)
    return (y_prompt, y_sample, jnp.stack(new_k, axis=1), jnp.stack(new_v, axis=1))
```

```python
import functools
import math

import jax
import jax.numpy as jnp
from jax import lax
from jax.experimental import pallas as pl
from jax.experimental.pallas import tpu as pltpu

F32 = jnp.float32
BF16 = jnp.bfloat16
I32 = jnp.int32

D_MODEL = 2048
HEAD_DIM = 128
N_HEADS = 16
N_KV_HEADS = 4
KV_GROUP = N_HEADS // N_KV_HEADS
GRID_W = 64
ROPE_THETA = 10000.0
AXIS_PAIRS = HEAD_DIM // 4
SGU_CHUNK = 128
SGU_GROUPS = 16
PEER_HEADS = 8
PEER_NKEYS = 128
PEER_TOPK = 16
PEER_HALF = 128
N_MOD = 6
EPS = 1e-6
N_MIXERS = 3
N_CTX_TOKENS = 16 * 256

LANES = 128
VMEM_LIMIT = 56 * 1024 * 1024

SEQ_ALIGN = 1024
MOD_ROWS = 8
TM = 1024
TN = 1024
TR_NM = 256
TR_ROUTE = 256
TM_PEER = 1024
A_STEP = 4
E_STEP = A_STEP * PEER_NKEYS


def _cparams(sem):
    return pltpu.CompilerParams(dimension_semantics=sem, vmem_limit_bytes=VMEM_LIMIT)


def _mod_row(tile_idx, tile_rows):
    per = SEQ_ALIGN // tile_rows
    return jnp.maximum(tile_idx // per - (N_CTX_TOKENS // SEQ_ALIGN - 1), 0)


def _gelu(x):
    c = math.sqrt(2.0 / math.pi)
    return 0.5 * x * (1.0 + jnp.tanh(c * (x + 0.044715 * (x * x * x))))


def _mod_kernel(c_ref, w_ref, b_ref, o_ref):
    c = c_ref[...]
    s = c * (1.0 / (1.0 + jnp.exp(-c)))
    o_ref[...] = jnp.dot(s, w_ref[...], preferred_element_type=F32) + b_ref[...]


def _modulation(cvec, w_mod, b_mod):
    depth, d, n = w_mod.shape
    tn = 1024
    return pl.pallas_call(
        _mod_kernel,
        out_shape=jax.ShapeDtypeStruct((depth, MOD_ROWS, n), F32),
        grid=(depth, n // tn),
        in_specs=[
            pl.BlockSpec((MOD_ROWS, d), lambda l, j: (0, 0)),
            pl.BlockSpec((None, d, tn), lambda l, j: (l, 0, j)),
            pl.BlockSpec((None, 1, tn), lambda l, j: (l, 0, j)),
        ],
        out_specs=pl.BlockSpec((None, MOD_ROWS, tn), lambda l, j: (l, 0, j)),
        compiler_params=_cparams(("parallel", "parallel")),
        name="modulation",
    )(cvec, w_mod, b_mod.reshape(depth, 1, n))


def _norm_mod_kernel(has_res, x_ref, *refs):
    if has_res:
        y_ref, gate_ref, g_ref, shift_ref, scale_ref, xo_ref, h_ref = refs
        x = x_ref[...] + gate_ref[...] * y_ref[...]
        xo_ref[...] = x
    else:
        g_ref, shift_ref, scale_ref, h_ref = refs
        x = x_ref[...]
    ms = jnp.mean(x * x, axis=-1, keepdims=True)
    y = x * lax.rsqrt(ms + EPS)
    y = y * g_ref[...]
    h_ref[...] = (y * (1.0 + scale_ref[...]) + shift_ref[...]).astype(h_ref.dtype)


def _norm_mod(x, y, mods, gate_ln, shift_ln, scale_ln, g):
    t, d = x.shape
    tr = TR_NM
    row_spec = pl.BlockSpec((tr, d), lambda i: (i, 0))

    def mod_spec(ln):
        return pl.BlockSpec((None, None, None, 1, d),
                            lambda i: (ln[0], _mod_row(i, tr), ln[1], 0, 0))

    g_spec = pl.BlockSpec((1, d), lambda i: (0, 0))
    has_res = y is not None
    if has_res:
        args = (x, y, mods, g.reshape(1, d), mods, mods)
        in_specs = [row_spec, row_spec, mod_spec(gate_ln), g_spec, mod_spec(shift_ln), mod_spec(scale_ln)]
        out_shape = (jax.ShapeDtypeStruct((t, d), F32), jax.ShapeDtypeStruct((t, d), BF16))
        out_specs = (row_spec, row_spec)
    else:
        args = (x, g.reshape(1, d), mods, mods)
        in_specs = [row_spec, g_spec, mod_spec(shift_ln), mod_spec(scale_ln)]
        out_shape = jax.ShapeDtypeStruct((t, d), BF16)
        out_specs = row_spec
    out = pl.pallas_call(
        functools.partial(_norm_mod_kernel, has_res),
        out_shape=out_shape, grid=(t // tr,), in_specs=in_specs, out_specs=out_specs,
        compiler_params=_cparams(("parallel",)), name="norm_mod",
    )(*args)
    return out if has_res else (x, out)


def _final_norm_kernel(x_ref, y_ref, gate_ref, g_ref, o_ref):
    x = x_ref[...] + gate_ref[...] * y_ref[...]
    ms = jnp.mean(x * x, axis=-1, keepdims=True)
    o_ref[...] = x * lax.rsqrt(ms + EPS) * g_ref[...]


def _final_norm(x, y, mods, gate_ln, g):
    t, d = x.shape
    tr = TR_NM
    row_spec = pl.BlockSpec((tr, d), lambda i: (i, 0))
    return pl.pallas_call(
        _final_norm_kernel,
        out_shape=jax.ShapeDtypeStruct((t, d), F32), grid=(t // tr,),
        in_specs=[row_spec, row_spec,
                  pl.BlockSpec((None, None, None, 1, d),
                               lambda i: (gate_ln[0], _mod_row(i, tr), gate_ln[1], 0, 0)),
                  pl.BlockSpec((1, d), lambda i: (0, 0))],
        out_specs=row_spec,
        compiler_params=_cparams(("parallel",)), name="final_norm",
    )(x, y, mods, g.reshape(1, d))


def _matmul_kernel(act, a_ref, w_ref, o_ref):
    acc = jnp.dot(a_ref[...], w_ref[...].astype(BF16), preferred_element_type=F32)
    if act == "gelu":
        acc = _gelu(acc)
    o_ref[...] = acc.astype(o_ref.dtype)


def _matmul(a, w_stack, layer, out_dtype=F32, act=None):
    m, k = a.shape
    n = w_stack.shape[2]
    tm = min(TM, m)
    return pl.pallas_call(
        functools.partial(_matmul_kernel, act),
        out_shape=jax.ShapeDtypeStruct((m, n), out_dtype),
        grid=(m // tm, n // TN),
        in_specs=[pl.BlockSpec((tm, k), lambda i, j: (i, 0)),
                  pl.BlockSpec((None, k, TN), lambda i, j: (layer, 0, j))],
        out_specs=pl.BlockSpec((tm, TN), lambda i, j: (i, j)),
        compiler_params=_cparams(("parallel", "arbitrary")), name="matmul",
    )(a, w_stack)


def _rope_tables(seq_len):
    rows = seq_len // GRID_W
    t = jnp.arange(seq_len)
    r = jnp.repeat(jnp.arange(rows), GRID_W).astype(F32)
    col = (t % GRID_W).astype(F32)
    inv = ROPE_THETA ** (-jnp.arange(AXIS_PAIRS, dtype=F32) / AXIS_PAIRS)
    ang = jnp.concatenate([r[:, None] * inv, col[:, None] * inv], axis=-1)
    cos, sin = jnp.cos(ang), jnp.sin(ang)
    cos_f = jnp.repeat(cos, 2, axis=-1)
    sin_s = jnp.stack([-sin, sin], axis=-1).reshape(seq_len, HEAD_DIM)
    return cos_f, sin_s


def _qk_post_kernel(n_ctx_tiles, qkv_ref, g_ref, cos_ref, sin_ref, q_ref, k_ref):
    is_lat = pl.program_id(0) >= n_ctx_tiles

    def heads(with_rope):
        if with_rope:
            cos, sin = cos_ref[...], sin_ref[...]
            even = (lax.broadcasted_iota(I32, cos.shape, 1) & 1) == 0
        for hd in range(N_HEADS + N_KV_HEADS):
            x = qkv_ref[:, hd * HEAD_DIM:(hd + 1) * HEAD_DIM]
            ms = jnp.mean(x * x, axis=-1, keepdims=True)
            y = x * lax.rsqrt(ms + EPS) * g_ref[:, hd * HEAD_DIM:(hd + 1) * HEAD_DIM]
            if with_rope:
                partner = jnp.where(even, pltpu.roll(y, HEAD_DIM - 1, 1), pltpu.roll(y, 1, 1))
                y = y * cos + partner * sin
            if hd < N_HEADS:
                q_ref[:, hd * HEAD_DIM:(hd + 1) * HEAD_DIM] = y.astype(q_ref.dtype)
            else:
                kh = hd - N_HEADS
                k_ref[:, kh * HEAD_DIM:(kh + 1) * HEAD_DIM] = y

    @pl.when(is_lat)
    def _():
        heads(True)

    @pl.when(jnp.logical_not(is_lat))
    def _():
        heads(False)


def _qk_post(qkv, g_full, cos_f, sin_s, n_ctx_tokens):
    t = qkv.shape[0]
    tr = 256
    nq, nk = N_HEADS * HEAD_DIM, N_KV_HEADS * HEAD_DIM
    lat_tiles = cos_f.shape[0] // tr
    n_ctx_tiles = n_ctx_tokens // tr

    def pos_map(i):
        return (jnp.maximum(i - n_ctx_tiles, 0) % lat_tiles, 0)

    return pl.pallas_call(
        functools.partial(_qk_post_kernel, n_ctx_tiles),
        out_shape=(jax.ShapeDtypeStruct((t, nq), BF16), jax.ShapeDtypeStruct((t, nk), F32)),
        grid=(t // tr,),
        in_specs=[pl.BlockSpec((tr, nq + nk), lambda i: (i, 0)),
                  pl.BlockSpec((1, nq + nk), lambda i: (0, 0)),
                  pl.BlockSpec((tr, HEAD_DIM), pos_map),
                  pl.BlockSpec((tr, HEAD_DIM), pos_map)],
        out_specs=(pl.BlockSpec((tr, nq), lambda i: (i, 0)), pl.BlockSpec((tr, nk), lambda i: (i, 0))),
        compiler_params=_cparams(("parallel",)), name="qk_post",
    )(qkv, g_full, cos_f, sin_s)


def _attn_kernel(has_cache, q_ref, k_ref, v_ref, *refs):
    if has_cache:
        ck_ref, cv_ref, o_ref = refs
    else:
        (o_ref,) = refs
    scale = HEAD_DIM ** -0.5
    k = k_ref[...].astype(BF16)
    v = v_ref[...].astype(BF16)
    if has_cache:
        ck = ck_ref[...].astype(BF16)
        cv = cv_ref[...].astype(BF16)
    nt = (((1,), (1,)), ((), ()))
    for g in range(KV_GROUP):
        q = q_ref[:, g * HEAD_DIM:(g + 1) * HEAD_DIM]
        s1 = lax.dot_general(q, k, nt, preferred_element_type=F32) * scale
        m = jnp.max(s1, axis=-1, keepdims=True)
        if has_cache:
            s2 = lax.dot_general(q, ck, nt, preferred_element_type=F32) * scale
            m = jnp.maximum(m, jnp.max(s2, axis=-1, keepdims=True))
        p1 = jnp.exp(s1 - m)
        l = jnp.sum(p1, axis=-1, keepdims=True)
        if has_cache:
            p2 = jnp.exp(s2 - m)
            l = l + jnp.sum(p2, axis=-1, keepdims=True)
        inv = 1.0 / l
        o = jnp.dot((p1 * inv).astype(BF16), v, preferred_element_type=F32)
        if has_cache:
            o = o + jnp.dot((p2 * inv).astype(BF16), cv, preferred_element_type=F32)
        o_ref[:, g * HEAD_DIM:(g + 1) * HEAD_DIM] = o.astype(o_ref.dtype)


def _attention(qn, kn, qkv, row0, n_seq, seq_len, cache_k=None, cache_v=None):
    tq = 256
    qb = seq_len // tq
    r0q = row0 // tq
    r0k = row0 // seq_len
    gw = KV_GROUP * HEAD_DIM
    v_col0 = (N_HEADS + N_KV_HEADS)
    has_cache = cache_k is not None
    in_specs = [
        pl.BlockSpec((tq, gw), lambda b, h, i: (r0q + b * qb + i, h)),
        pl.BlockSpec((seq_len, HEAD_DIM), lambda b, h, i: (r0k + b, h)),
        pl.BlockSpec((seq_len, HEAD_DIM), lambda b, h, i: (r0k + b, v_col0 + h)),
    ]
    args = [qn, kn, qkv]
    if has_cache:
        past = cache_k.shape[2]
        cspec = pl.BlockSpec((None, None, past, HEAD_DIM), lambda b, h, i: (b, h, 0, 0))
        in_specs += [cspec, cspec]
        args += [cache_k, cache_v]
    return pl.pallas_call(
        functools.partial(_attn_kernel, has_cache),
        out_shape=jax.ShapeDtypeStruct((n_seq * seq_len, N_HEADS * HEAD_DIM), BF16),
        grid=(n_seq, N_KV_HEADS, qb),
        in_specs=in_specs,
        out_specs=pl.BlockSpec((tq, gw), lambda b, h, i: (b * qb + i, h)),
        compiler_params=_cparams(("parallel", "parallel", "arbitrary")), name="attention",
    )(*args)


def _sgu_mid_kernel(u_ref, v_ref, g_ref, ws_ref, b_ref, o_ref):
    v = v_ref[...]
    ms = jnp.mean(v * v, axis=-1, keepdims=True)
    vn = (v * lax.rsqrt(ms + EPS) * g_ref[...]).astype(BF16)
    for g in range(SGU_GROUPS):
        sl = slice(g * LANES, (g + 1) * LANES)
        mixed = jnp.dot(ws_ref[g].astype(BF16), vn[:, sl], preferred_element_type=F32) + b_ref[:, sl]
        o_ref[:, sl] = (u_ref[:, sl] * mixed).astype(o_ref.dtype)


def _sgu_mid(z, g_v, w_s, b_full):
    t = z.shape[0]
    w = z.shape[1] // 2
    ch = SGU_CHUNK
    return pl.pallas_call(
        _sgu_mid_kernel,
        out_shape=jax.ShapeDtypeStruct((t, w), BF16),
        grid=(t // ch,),
        in_specs=[pl.BlockSpec((ch, w), lambda i: (i, 0)),
                  pl.BlockSpec((ch, w), lambda i: (i, 1)),
                  pl.BlockSpec((1, w), lambda i: (0, 0)),
                  pl.BlockSpec((SGU_GROUPS, ch, ch), lambda i: (0, 0, 0)),
                  pl.BlockSpec((ch, w), lambda i: (0, 0))],
        out_specs=pl.BlockSpec((ch, w), lambda i: (i, 0)),
        compiler_params=_cparams(("parallel",)), name="sgu_mid",
    )(z, z, g_v.reshape(1, w), w_s, b_full)


def _conv_mid_kernel(n_ctx_tiles, ctx_len, lat_len, bg_ref, cg_ref, z_ref, w_ref, o_ref):
    z = cg_ref[...] * z_ref[...]
    rows = z.shape[0]
    seq_len = jnp.where(pl.program_id(0) < n_ctx_tiles, ctx_len, lat_len)
    pos = lax.broadcasted_iota(I32, z.shape, 0) & (seq_len - 1)
    prev = jnp.where(pos == 0, 0.0, pltpu.roll(z, 1, 0))
    nxt = jnp.where(pos == seq_len - 1, 0.0, pltpu.roll(z, rows - 1, 0))
    zc = w_ref[0:1, :] * prev + w_ref[1:2, :] * z + w_ref[2:3, :] * nxt
    o_ref[...] = (bg_ref[...] * zc).astype(o_ref.dtype)


def _conv_mid(y, conv_w, n_ctx_tokens, ctx_len, lat_len):
    t = y.shape[0]
    d = y.shape[1] // 3
    tr, tc = SEQ_ALIGN, 512
    nc = d // tc
    return pl.pallas_call(
        functools.partial(_conv_mid_kernel, n_ctx_tokens // tr, ctx_len, lat_len),
        out_shape=jax.ShapeDtypeStruct((t, d), BF16),
        grid=(t // tr, nc),
        in_specs=[pl.BlockSpec((tr, tc), lambda i, j: (i, j)),
                  pl.BlockSpec((tr, tc), lambda i, j: (i, nc + j)),
                  pl.BlockSpec((tr, tc), lambda i, j: (i, 2 * nc + j)),
                  pl.BlockSpec((8, tc), lambda i, j: (0, j))],
        out_specs=pl.BlockSpec((tr, tc), lambda i, j: (i, j)),
        compiler_params=_cparams(("parallel", "parallel")), name="conv_mid",
    )(y, y, y, conv_w)


_CAND = [(i, j) for i in range(PEER_TOPK) for j in range(PEER_TOPK) if (i + 1) * (j + 1) <= PEER_TOPK]


def _top_list(s, k, exact_ties):
    n = s.shape[0]
    if exact_ties:
        iota = lax.broadcasted_iota(I32, s.shape, 0)
    vals = []
    for r in range(k):
        m = jnp.max(s, axis=0, keepdims=True)
        hit = s == m
        if exact_ties:
            idx = jnp.min(jnp.where(hit, iota, n), axis=0, keepdims=True)
            hit = iota == idx
        s = jnp.where(hit, -_TAKEN * (1.0 + r / k), s)
        vals.append(m)
    rank = jnp.where(s < -0.5 * _TAKEN, (s * (-1.0 / _TAKEN) - 1.0) * k, float(k))
    return vals, rank


_TAKEN = 2.0 ** 126


def _ranked_count_off(rank, k):
    return jnp.sum(jnp.where(rank < k, 1, 0), axis=0, keepdims=True) != k


def _bf16_pair_word(x):
    hi = pltpu.bitcast(x.astype(BF16).astype(F32), I32)
    return hi | lax.shift_right_logical(hi, 16)


def _route_head(h, exact_halves, q_ref, keys_ref, len_ref, e1_ref, r2_ref, e2_ref):
    nt = (((1,), (1,)), ((), ()))
    k = PEER_TOPK
    sc = []
    for c in range(2):
        col = (2 * h + c) * PEER_HALF
        sc.append(lax.dot_general(keys_ref[h, c].astype(BF16), q_ref[:, col:col + PEER_HALF], nt,
                                  preferred_element_type=F32))
    v1, rank1 = _top_list(sc[0], k, exact_halves)
    v2, rank2 = _top_list(sc[1], k, exact_halves)
    cand = jnp.concatenate([v1[i] + v2[j] for (i, j) in _CAND], axis=0)
    _, crank = _top_list(cand, k, True)
    sel = crank < k
    ev1 = [jnp.exp(v - v1[0]) for v in v1]
    ev2 = [jnp.exp(v - v2[0]) for v in v2]
    zsum = jnp.zeros_like(v1[0])
    row_len = [jnp.zeros(v1[0].shape, F32) for _ in range(k)]
    for r, (i, j) in enumerate(_CAND):
        s_r = sel[r:r + 1, :]
        zsum = zsum + jnp.where(s_r, ev1[i] * ev2[j], 0.0)
        row_len[i] = row_len[i] + jnp.where(s_r, 1.0, 0.0)
    len_a = jnp.zeros(rank1.shape, F32)
    for i in range(k):
        len_a = jnp.where(rank1 == i, row_len[i], len_a)
    len_ref[:, h, :] = _bf16_pair_word(len_a)
    e1_ref[:, h, :] = _bf16_pair_word(jnp.exp(sc[0] - v1[0]))
    r2_ref[h] = pltpu.bitcast(rank2.astype(BF16), I32)
    e2_ref[h] = pltpu.bitcast((jnp.exp(sc[1] - v2[0]) / zsum).astype(BF16), I32)
    if exact_halves:
        return None
    return _ranked_count_off(rank1, k) | _ranked_count_off(rank2, k)


def _route_kernel(q_ref, keys_ref, *out_refs):
    tie = None
    for h in range(PEER_HEADS):
        t = _route_head(h, False, q_ref, keys_ref, *out_refs)
        tie = t if tie is None else tie | t

    @pl.when(jnp.max(jnp.where(tie, 1, 0)) > 0)
    def _():
        for h in range(PEER_HEADS):
            _route_head(h, True, q_ref, keys_ref, *out_refs)


def _route(q, keys_stack, layer):
    t = q.shape[0]
    tr = TR_ROUTE
    shp1 = (PEER_NKEYS, PEER_HEADS, t)
    spec1 = pl.BlockSpec((PEER_NKEYS, PEER_HEADS, tr), lambda i: (0, 0, i))
    shp2 = (PEER_HEADS, PEER_NKEYS // 2, t)
    spec2 = pl.BlockSpec((PEER_HEADS, PEER_NKEYS // 2, tr), lambda i: (0, 0, i))
    return pl.pallas_call(
        _route_kernel,
        out_shape=(jax.ShapeDtypeStruct(shp1, I32), jax.ShapeDtypeStruct(shp1, I32),
                   jax.ShapeDtypeStruct(shp2, I32), jax.ShapeDtypeStruct(shp2, I32)),
        grid=(t // tr,),
        in_specs=[pl.BlockSpec((tr, q.shape[1]), lambda i: (i, 0)),
                  pl.BlockSpec((None,) + keys_stack.shape[1:], lambda i: (layer, 0, 0, 0, 0))],
        out_specs=(spec1, spec1, spec2, spec2),
        compiler_params=_cparams(("parallel",)), name="peer_route",
    )(q, keys_stack)


def _gelu_gate(x, g):
    c = 2.0 * math.sqrt(2.0 / math.pi) * math.log2(math.e)
    e = jnp.exp2(x * (-c - (c * 0.044715) * (x * x)))
    return (x * g) / (1.0 + e)


def _bf16_rows(word_row):
    words = jnp.broadcast_to(word_row, (PEER_NKEYS // 2, word_row.shape[1]))
    return pltpu.bitcast(words, BF16)


def _peer_dense_kernel(n_steps, h_ref, len_ref, e1_ref, r2_ref, e2_ref, u_ref, v_ref, y_ref, s0, s1, ga):
    step = pl.program_id(0)
    nt = (((1,), (1,)), ((), ()))
    tm = h_ref.shape[0]
    d = v_ref.shape[1]
    half_e = E_STEP // 2

    @pl.when(step == 0)
    def _():
        s0[...] = jnp.zeros_like(s0)
        s1[...] = jnp.zeros_like(s1)

    @pl.when((step == 0) | (lax.rem(step + n_steps - 1, n_steps) == 0))
    def _():
        y_ref[...] = jnp.zeros_like(y_ref)

    def body(s_new, s_old):
        def scores(half):
            rows = slice(half * half_e, (half + 1) * half_e)
            s_new[:, rows] = lax.dot_general(h_ref[...], u_ref[rows, :].astype(BF16), nt,
                                             preferred_element_type=F32)

        def gates(r, c):
            tok = slice(c * LANES, (c + 1) * LANES)
            gt = None
            for hd in range(PEER_HEADS):
                r2 = pltpu.bitcast(r2_ref[hd, :, tok], BF16)
                e2 = pltpu.bitcast(e2_ref[hd, :, tok], BF16)
                hit = r2 < _bf16_rows(len_ref[r, hd:hd + 1, tok])
                term = jnp.where(hit, e2, 0.0) * _bf16_rows(e1_ref[r, hd:hd + 1, tok])
                gt = term if gt is None else gt + term
            cols = slice(r * PEER_NKEYS, (r + 1) * PEER_NKEYS)
            ga[tok, cols] = _gelu_gate(s_old[tok, cols].astype(BF16), gt.T)

        def contract(piece, n_pieces):
            w = d // n_pieces
            cols = slice(piece * w, (piece + 1) * w)
            y_ref[:, cols] += jnp.dot(ga[...], v_ref[:, cols].astype(BF16), preferred_element_type=F32)

        scores(0)
        scores(1)
        for r in range(A_STEP):
            for c in range(tm // LANES):
                gates(r, c)
        for piece in range(4):
            contract(piece, 4)

    even = lax.rem(step, 2) == 0

    @pl.when(even)
    def _():
        body(s0, s1)

    @pl.when(jnp.logical_not(even))
    def _():
        body(s1, s0)


def _peer_dense(h, route, u_stack, v_stack, layer):
    t, d = h.shape
    e = u_stack.shape[1]
    tm = min(TM_PEER, t)
    n_steps = e // E_STEP
    len_a, e1, r2, e2 = route
    n_pairs = (t // tm) * n_steps

    def pair(g, lag):
        p = jnp.clip(g - lag, 0, n_pairs - 1)
        return p // n_steps, p % n_steps

    aspec = pl.BlockSpec((A_STEP, PEER_HEADS, tm), lambda g: (pair(g, 1)[1], 0, pair(g, 1)[0]))
    bspec = pl.BlockSpec((PEER_HEADS, PEER_NKEYS // 2, tm), lambda g: (0, 0, pair(g, 1)[0]))
    return pl.pallas_call(
        functools.partial(_peer_dense_kernel, n_steps),
        out_shape=jax.ShapeDtypeStruct((t, d), F32),
        grid=(n_pairs + 1,),
        in_specs=[pl.BlockSpec((tm, d), lambda g: (pair(g, 0)[0], 0)),
                  aspec, aspec, bspec, bspec,
                  pl.BlockSpec((None, E_STEP, d), lambda g: (layer, pair(g, 0)[1], 0)),
                  pl.BlockSpec((None, E_STEP, d), lambda g: (layer, pair(g, 1)[1], 0))],
        out_specs=pl.BlockSpec((tm, d), lambda g: (pair(g, 1)[0], 0)),
        scratch_shapes=[pltpu.VMEM((tm, E_STEP), F32), pltpu.VMEM((tm, E_STEP), F32),
                        pltpu.VMEM((tm, E_STEP), BF16)],
        compiler_params=_cparams(("arbitrary",)), name="peer_dense",
    )(h, len_a, e1, r2, e2, u_stack, v_stack)


def kernel(x_prompt, x_sample, cache_k, cache_v, c, c_ctx, norm_g, final_g, w_mod, b_mod, attn_w_qkv, attn_q_g, attn_k_g, attn_w_o, sgu_w_in, sgu_g, sgu_w_s, sgu_b, sgu_w_out, conv_w_in, conv_w, conv_w_out, peer_w_q, peer_keys, peer_u, peer_v):
    nb, sl, d = x_prompt.shape
    db, dsl, _ = x_sample.shape
    depth = w_mod.shape[0]
    n_ctx = nb * sl
    assert d == D_MODEL and dsl == SEQ_ALIGN and n_ctx == N_CTX_TOKENS and SEQ_ALIGN % sl == 0
    assert 1 + db <= MOD_ROWS

    x = jnp.concatenate([x_prompt.reshape(n_ctx, d), x_sample.reshape(db * dsl, d)], axis=0)
    cvec = jnp.concatenate([c_ctx[None, :], c, jnp.zeros((MOD_ROWS - 1 - db, d), F32)], axis=0)
    mods = _modulation(cvec, w_mod, b_mod).reshape(depth, MOD_ROWS, N_MOD, 1, d)

    cos_f, sin_s = _rope_tables(dsl)
    nq, nk = N_HEADS * HEAD_DIM, N_KV_HEADS * HEAD_DIM
    new_k, new_v = [], []
    y = None
    for i in range(depth):
        kind, j = i % N_MIXERS, i // N_MIXERS
        x, h = _norm_mod(x, y, mods, (i - 1, 5), (i, 0), (i, 1), norm_g[i, 0])
        if kind == 0:
            qkv = _matmul(h, attn_w_qkv, j)
            g_full = jnp.concatenate([jnp.tile(attn_q_g[j], N_HEADS), jnp.tile(attn_k_g[j], N_KV_HEADS)])
            qn, kn = _qk_post(qkv, g_full.reshape(1, nq + nk), cos_f, sin_s, n_ctx)
            new_k.append(kn[:n_ctx].reshape(nb, sl, N_KV_HEADS, HEAD_DIM))
            new_v.append(qkv[:n_ctx, nq + nk:].reshape(nb, sl, N_KV_HEADS, HEAD_DIM))
            o_ctx = _attention(qn, kn, qkv, 0, nb, sl)
            ck = jnp.transpose(cache_k[:, j], (0, 2, 1, 3))
            cv = jnp.transpose(cache_v[:, j], (0, 2, 1, 3))
            o_lat = _attention(qn, kn, qkv, n_ctx, db, dsl, ck, cv)
            o = _matmul(jnp.concatenate([o_ctx, o_lat], axis=0), attn_w_o, j)
        elif kind == 1:
            z = _matmul(h, sgu_w_in, j, act="gelu")
            b_full = jnp.repeat(sgu_b[j].T, SGU_CHUNK, axis=1)
            o = _matmul(_sgu_mid(z, sgu_g[j], sgu_w_s[j], b_full), sgu_w_out, j)
        else:
            yc = _matmul(h, conv_w_in, j)
            cw = jnp.concatenate([conv_w[j], jnp.zeros((8 - conv_w.shape[1], d), F32)], axis=0)
            o = _matmul(_conv_mid(yc, cw, n_ctx, sl, dsl), conv_w_out, j)
        x, h = _norm_mod(x, o, mods, (i, 2), (i, 3), (i, 4), norm_g[i, 1])
        q = _matmul(h, peer_w_q, i, out_dtype=BF16)
        y = _peer_dense(h, _route(q, peer_keys, i), peer_u, peer_v, i)

    out = _final_norm(x, y, mods, (depth - 1, 5), final_g)
    y_prompt = out[:n_ctx].reshape(nb, sl, d)
    y_sample = out[n_ctx:].reshape(db, dsl, d)
    return (y_prompt, y_sample, jnp.stack(new_k, axis=1), jnp.stack(new_v, axis=1))
```

```python
import functools
import math

import jax
import jax.numpy as jnp
from jax import lax
from jax.experimental import pallas as pl
from jax.experimental.pallas import tpu as pltpu

F32 = jnp.float32
BF16 = jnp.bfloat16
I32 = jnp.int32

D_MODEL = 2048
HEAD_DIM = 128
N_HEADS = 16
N_KV_HEADS = 4
KV_GROUP = N_HEADS // N_KV_HEADS
GRID_W = 64
ROPE_THETA = 10000.0
AXIS_PAIRS = HEAD_DIM // 4
SGU_CHUNK = 128
SGU_GROUPS = 16
PEER_HEADS = 8
PEER_NKEYS = 128
PEER_TOPK = 16
PEER_HALF = 128
N_MOD = 6
EPS = 1e-6
N_MIXERS = 3
N_CTX_TOKENS = 16 * 256

LANES = 128
VMEM_LIMIT = 56 * 1024 * 1024

SEQ_ALIGN = 1024
MOD_ROWS = 8
TM = 1024
TN = 1024
TR_NM = 512
ATTN_STACK_CACHE = 4
TR_ROUTE = 256
TM_PEER = 1024
A_STEP = 4
E_STEP = A_STEP * PEER_NKEYS


def _cparams(sem):
    return pltpu.CompilerParams(dimension_semantics=sem, vmem_limit_bytes=VMEM_LIMIT)


def _mod_row(tile_idx, tile_rows):
    per = SEQ_ALIGN // tile_rows
    return jnp.maximum(tile_idx // per - (N_CTX_TOKENS // SEQ_ALIGN - 1), 0)


def _gelu(x):
    c = math.sqrt(2.0 / math.pi)
    return 0.5 * x * (1.0 + jnp.tanh(c * (x + 0.044715 * (x * x * x))))


def _mod_kernel(c_ref, w_ref, b_ref, o_ref):
    c = c_ref[...]
    s = c * (1.0 / (1.0 + jnp.exp(-c)))
    o_ref[...] = jnp.dot(s, w_ref[...], preferred_element_type=F32) + b_ref[...]


def _modulation(cvec, w_mod, b_mod):
    depth, d, n = w_mod.shape
    tn = 1024
    return pl.pallas_call(
        _mod_kernel,
        out_shape=jax.ShapeDtypeStruct((depth, MOD_ROWS, n), F32),
        grid=(depth, n // tn),
        in_specs=[
            pl.BlockSpec((MOD_ROWS, d), lambda l, j: (0, 0)),
            pl.BlockSpec((None, d, tn), lambda l, j: (l, 0, j)),
            pl.BlockSpec((None, 1, tn), lambda l, j: (l, 0, j)),
        ],
        out_specs=pl.BlockSpec((None, MOD_ROWS, tn), lambda l, j: (l, 0, j)),
        compiler_params=_cparams(("parallel", "parallel")),
        name="modulation",
    )(cvec, w_mod, b_mod.reshape(depth, 1, n))


def _norm_mod_kernel(has_res, x_ref, *refs):
    if has_res:
        y_ref, gate_ref, g_ref, shift_ref, scale_ref, xo_ref, h_ref = refs
        x = x_ref[...] + gate_ref[...] * y_ref[...]
        xo_ref[...] = x
    else:
        g_ref, shift_ref, scale_ref, h_ref = refs
        x = x_ref[...]
    ms = jnp.mean(x * x, axis=-1, keepdims=True)
    y = x * lax.rsqrt(ms + EPS)
    y = y * g_ref[...]
    h_ref[...] = (y * (1.0 + scale_ref[...]) + shift_ref[...]).astype(h_ref.dtype)


def _norm_mod(x, y, mods, gate_ln, shift_ln, scale_ln, g):
    t, d = x.shape
    tr = TR_NM
    row_spec = pl.BlockSpec((tr, d), lambda i: (i, 0))

    def mod_spec(ln):
        return pl.BlockSpec((None, None, None, 1, d),
                            lambda i: (ln[0], _mod_row(i, tr), ln[1], 0, 0))

    g_spec = pl.BlockSpec((1, d), lambda i: (0, 0))
    has_res = y is not None
    if has_res:
        args = (x, y, mods, g.reshape(1, d), mods, mods)
        in_specs = [row_spec, row_spec, mod_spec(gate_ln), g_spec, mod_spec(shift_ln), mod_spec(scale_ln)]
        out_shape = (jax.ShapeDtypeStruct((t, d), F32), jax.ShapeDtypeStruct((t, d), BF16))
        out_specs = (row_spec, row_spec)
    else:
        args = (x, g.reshape(1, d), mods, mods)
        in_specs = [row_spec, g_spec, mod_spec(shift_ln), mod_spec(scale_ln)]
        out_shape = jax.ShapeDtypeStruct((t, d), BF16)
        out_specs = row_spec
    out = pl.pallas_call(
        functools.partial(_norm_mod_kernel, has_res),
        out_shape=out_shape, grid=(t // tr,), in_specs=in_specs, out_specs=out_specs,
        compiler_params=_cparams(("parallel",)), name="norm_mod",
    )(*args)
    return out if has_res else (x, out)


def _final_norm_kernel(x_ref, y_ref, gate_ref, g_ref, o_ref):
    x = x_ref[...] + gate_ref[...] * y_ref[...]
    ms = jnp.mean(x * x, axis=-1, keepdims=True)
    o_ref[...] = x * lax.rsqrt(ms + EPS) * g_ref[...]


def _final_norm(x, y, mods, gate_ln, g):
    t, d = x.shape
    tr = TR_NM
    row_spec = pl.BlockSpec((tr, d), lambda i: (i, 0))
    return pl.pallas_call(
        _final_norm_kernel,
        out_shape=jax.ShapeDtypeStruct((t, d), F32), grid=(t // tr,),
        in_specs=[row_spec, row_spec,
                  pl.BlockSpec((None, None, None, 1, d),
                               lambda i: (gate_ln[0], _mod_row(i, tr), gate_ln[1], 0, 0)),
                  pl.BlockSpec((1, d), lambda i: (0, 0))],
        out_specs=row_spec,
        compiler_params=_cparams(("parallel",)), name="final_norm",
    )(x, y, mods, g.reshape(1, d))


def _matmul_kernel(act, a_ref, w_ref, o_ref):
    acc = jnp.dot(a_ref[...], w_ref[...].astype(BF16), preferred_element_type=F32)
    if act == "gelu":
        acc = _gelu(acc)
    o_ref[...] = acc.astype(o_ref.dtype)


def _matmul(a, w_stack, layer, out_dtype=F32, act=None):
    m, k = a.shape
    n = w_stack.shape[2]
    tm = min(TM, m)
    return pl.pallas_call(
        functools.partial(_matmul_kernel, act),
        out_shape=jax.ShapeDtypeStruct((m, n), out_dtype),
        grid=(m // tm, n // TN),
        in_specs=[pl.BlockSpec((tm, k), lambda i, j: (i, 0)),
                  pl.BlockSpec((None, k, TN), lambda i, j: (layer, 0, j))],
        out_specs=pl.BlockSpec((tm, TN), lambda i, j: (i, j)),
        compiler_params=_cparams(("parallel", "arbitrary")), name="matmul",
    )(a, w_stack)


def _rope_tables(seq_len):
    rows = seq_len // GRID_W
    t = jnp.arange(seq_len)
    r = jnp.repeat(jnp.arange(rows), GRID_W).astype(F32)
    col = (t % GRID_W).astype(F32)
    inv = ROPE_THETA ** (-jnp.arange(AXIS_PAIRS, dtype=F32) / AXIS_PAIRS)
    ang = jnp.concatenate([r[:, None] * inv, col[:, None] * inv], axis=-1)
    cos, sin = jnp.cos(ang), jnp.sin(ang)
    cos_f = jnp.repeat(cos, 2, axis=-1)
    sin_s = jnp.stack([-sin, sin], axis=-1).reshape(seq_len, HEAD_DIM)
    return cos_f, sin_s


def _qk_post_kernel(n_ctx_tiles, qkv_ref, g_ref, cos_ref, sin_ref, q_ref, k_ref):
    is_lat = pl.program_id(0) >= n_ctx_tiles

    def heads(with_rope):
        if with_rope:
            cos, sin = cos_ref[...], sin_ref[...]
            even = (lax.broadcasted_iota(I32, cos.shape, 1) & 1) == 0
        for hd in range(N_HEADS + N_KV_HEADS):
            x = qkv_ref[:, hd * HEAD_DIM:(hd + 1) * HEAD_DIM]
            ms = jnp.mean(x * x, axis=-1, keepdims=True)
            y = x * lax.rsqrt(ms + EPS) * g_ref[:, hd * HEAD_DIM:(hd + 1) * HEAD_DIM]
            if with_rope:
                partner = jnp.where(even, pltpu.roll(y, HEAD_DIM - 1, 1), pltpu.roll(y, 1, 1))
                y = y * cos + partner * sin
            if hd < N_HEADS:
                q_ref[:, hd * HEAD_DIM:(hd + 1) * HEAD_DIM] = y.astype(q_ref.dtype)
            else:
                kh = hd - N_HEADS
                k_ref[:, kh * HEAD_DIM:(kh + 1) * HEAD_DIM] = y

    @pl.when(is_lat)
    def _():
        heads(True)

    @pl.when(jnp.logical_not(is_lat))
    def _():
        heads(False)


def _qk_post(qkv, g_full, cos_f, sin_s, n_ctx_tokens):
    t = qkv.shape[0]
    tr = 256
    nq, nk = N_HEADS * HEAD_DIM, N_KV_HEADS * HEAD_DIM
    lat_tiles = cos_f.shape[0] // tr
    n_ctx_tiles = n_ctx_tokens // tr

    def pos_map(i):
        return (jnp.maximum(i - n_ctx_tiles, 0) % lat_tiles, 0)

    return pl.pallas_call(
        functools.partial(_qk_post_kernel, n_ctx_tiles),
        out_shape=(jax.ShapeDtypeStruct((t, nq), BF16), jax.ShapeDtypeStruct((t, nk), F32)),
        grid=(t // tr,),
        in_specs=[pl.BlockSpec((tr, nq + nk), lambda i: (i, 0)),
                  pl.BlockSpec((1, nq + nk), lambda i: (0, 0)),
                  pl.BlockSpec((tr, HEAD_DIM), pos_map),
                  pl.BlockSpec((tr, HEAD_DIM), pos_map)],
        out_specs=(pl.BlockSpec((tr, nq), lambda i: (i, 0)), pl.BlockSpec((tr, nk), lambda i: (i, 0))),
        compiler_params=_cparams(("parallel",)), name="qk_post",
    )(qkv, g_full, cos_f, sin_s)


def _attn_kernel(has_cache, q_ref, k_ref, v_ref, *refs):
    if has_cache:
        ck_ref, cv_ref, o_ref = refs
    else:
        (o_ref,) = refs
    scale = HEAD_DIM ** -0.5
    k = k_ref[...].astype(BF16)
    v = v_ref[...].astype(BF16)
    if has_cache:
        ck = ck_ref[...].astype(BF16)
        cv = cv_ref[...].astype(BF16)
    nt = (((1,), (1,)), ((), ()))
    tq = q_ref.shape[0]
    n_stack = ATTN_STACK_CACHE if has_cache else KV_GROUP
    for g0 in range(0, KV_GROUP, n_stack):
        heads = range(g0, g0 + n_stack)
        q = jnp.concatenate([q_ref[:, g * HEAD_DIM:(g + 1) * HEAD_DIM] for g in heads], axis=0)
        s1 = lax.dot_general(q, k, nt, preferred_element_type=F32) * scale
        m = jnp.max(s1, axis=-1, keepdims=True)
        if has_cache:
            s2 = lax.dot_general(q, ck, nt, preferred_element_type=F32) * scale
            m = jnp.maximum(m, jnp.max(s2, axis=-1, keepdims=True))
        p1 = jnp.exp(s1 - m)
        l = jnp.sum(p1, axis=-1, keepdims=True)
        if has_cache:
            p2 = jnp.exp(s2 - m)
            l = l + jnp.sum(p2, axis=-1, keepdims=True)
        inv = 1.0 / l
        o = jnp.dot((p1 * inv).astype(BF16), v, preferred_element_type=F32)
        if has_cache:
            o = o + jnp.dot((p2 * inv).astype(BF16), cv, preferred_element_type=F32)
        for n, g in enumerate(heads):
            o_ref[:, g * HEAD_DIM:(g + 1) * HEAD_DIM] = o[n * tq:(n + 1) * tq].astype(o_ref.dtype)


def _attention(qn, kn, qkv, row0, n_seq, seq_len, cache_k=None, cache_v=None):
    tq = 256
    qb = seq_len // tq
    r0q = row0 // tq
    r0k = row0 // seq_len
    gw = KV_GROUP * HEAD_DIM
    v_col0 = (N_HEADS + N_KV_HEADS)
    has_cache = cache_k is not None
    in_specs = [
        pl.BlockSpec((tq, gw), lambda b, h, i: (r0q + b * qb + i, h)),
        pl.BlockSpec((seq_len, HEAD_DIM), lambda b, h, i: (r0k + b, h)),
        pl.BlockSpec((seq_len, HEAD_DIM), lambda b, h, i: (r0k + b, v_col0 + h)),
    ]
    args = [qn, kn, qkv]
    if has_cache:
        past = cache_k.shape[2]
        cspec = pl.BlockSpec((None, None, past, HEAD_DIM), lambda b, h, i: (b, h, 0, 0))
        in_specs += [cspec, cspec]
        args += [cache_k, cache_v]
    return pl.pallas_call(
        functools.partial(_attn_kernel, has_cache),
        out_shape=jax.ShapeDtypeStruct((n_seq * seq_len, N_HEADS * HEAD_DIM), BF16),
        grid=(n_seq, N_KV_HEADS, qb),
        in_specs=in_specs,
        out_specs=pl.BlockSpec((tq, gw), lambda b, h, i: (b * qb + i, h)),
        compiler_params=_cparams(("parallel", "parallel", "arbitrary")), name="attention",
    )(*args)


def _sgu_mid_kernel(u_ref, v_ref, g_ref, ws_ref, b_ref, o_ref):
    v = v_ref[...]
    ms = jnp.mean(v * v, axis=-1, keepdims=True)
    vn = (v * lax.rsqrt(ms + EPS) * g_ref[...]).astype(BF16)
    for g in range(SGU_GROUPS):
        sl = slice(g * LANES, (g + 1) * LANES)
        mixed = jnp.dot(ws_ref[g].astype(BF16), vn[:, sl], preferred_element_type=F32) + b_ref[:, sl]
        o_ref[:, sl] = (u_ref[:, sl] * mixed).astype(o_ref.dtype)


def _sgu_mid(z, g_v, w_s, b_full):
    t = z.shape[0]
    w = z.shape[1] // 2
    ch = SGU_CHUNK
    return pl.pallas_call(
        _sgu_mid_kernel,
        out_shape=jax.ShapeDtypeStruct((t, w), BF16),
        grid=(t // ch,),
        in_specs=[pl.BlockSpec((ch, w), lambda i: (i, 0)),
                  pl.BlockSpec((ch, w), lambda i: (i, 1)),
                  pl.BlockSpec((1, w), lambda i: (0, 0)),
                  pl.BlockSpec((SGU_GROUPS, ch, ch), lambda i: (0, 0, 0)),
                  pl.BlockSpec((ch, w), lambda i: (0, 0))],
        out_specs=pl.BlockSpec((ch, w), lambda i: (i, 0)),
        compiler_params=_cparams(("parallel",)), name="sgu_mid",
    )(z, z, g_v.reshape(1, w), w_s, b_full)


def _conv_mid_kernel(n_ctx_tiles, ctx_len, lat_len, bg_ref, cg_ref, z_ref, w_ref, o_ref):
    z = cg_ref[...] * z_ref[...]
    rows = z.shape[0]
    seq_len = jnp.where(pl.program_id(0) < n_ctx_tiles, ctx_len, lat_len)
    pos = lax.broadcasted_iota(I32, z.shape, 0) & (seq_len - 1)
    prev = jnp.where(pos == 0, 0.0, pltpu.roll(z, 1, 0))
    nxt = jnp.where(pos == seq_len - 1, 0.0, pltpu.roll(z, rows - 1, 0))
    zc = w_ref[0:1, :] * prev + w_ref[1:2, :] * z + w_ref[2:3, :] * nxt
    o_ref[...] = (bg_ref[...] * zc).astype(o_ref.dtype)


def _conv_mid(y, conv_w, n_ctx_tokens, ctx_len, lat_len):
    t = y.shape[0]
    d = y.shape[1] // 3
    tr, tc = SEQ_ALIGN, 512
    nc = d // tc
    return pl.pallas_call(
        functools.partial(_conv_mid_kernel, n_ctx_tokens // tr, ctx_len, lat_len),
        out_shape=jax.ShapeDtypeStruct((t, d), BF16),
        grid=(t // tr, nc),
        in_specs=[pl.BlockSpec((tr, tc), lambda i, j: (i, j)),
                  pl.BlockSpec((tr, tc), lambda i, j: (i, nc + j)),
                  pl.BlockSpec((tr, tc), lambda i, j: (i, 2 * nc + j)),
                  pl.BlockSpec((8, tc), lambda i, j: (0, j))],
        out_specs=pl.BlockSpec((tr, tc), lambda i, j: (i, j)),
        compiler_params=_cparams(("parallel", "parallel")), name="conv_mid",
    )(y, y, y, conv_w)


_CAND = [(i, j) for i in range(PEER_TOPK) for j in range(PEER_TOPK) if (i + 1) * (j + 1) <= PEER_TOPK]


def _top_list(s, k, exact_ties):
    n = s.shape[0]
    if exact_ties:
        iota = lax.broadcasted_iota(I32, s.shape, 0)
    vals = []
    for r in range(k):
        m = jnp.max(s, axis=0, keepdims=True)
        hit = s == m
        if exact_ties:
            idx = jnp.min(jnp.where(hit, iota, n), axis=0, keepdims=True)
            hit = iota == idx
        s = jnp.where(hit, -_TAKEN * (1.0 + r / k), s)
        vals.append(m)
    rank = jnp.where(s < -0.5 * _TAKEN, (s * (-1.0 / _TAKEN) - 1.0) * k, float(k))
    return vals, rank


_TAKEN = 2.0 ** 126


def _ranked_count_off(rank, k):
    return jnp.sum(jnp.where(rank < k, 1, 0), axis=0, keepdims=True) != k


def _bf16_pair_word(x):
    hi = pltpu.bitcast(x.astype(BF16).astype(F32), I32)
    return hi | lax.shift_right_logical(hi, 16)


def _route_head(h, exact_halves, q_ref, keys_ref, len_ref, e1_ref, r2_ref, e2_ref):
    nt = (((1,), (1,)), ((), ()))
    k = PEER_TOPK
    sc = []
    for c in range(2):
        col = (2 * h + c) * PEER_HALF
        sc.append(lax.dot_general(keys_ref[h, c].astype(BF16), q_ref[:, col:col + PEER_HALF], nt,
                                  preferred_element_type=F32))
    v1, rank1 = _top_list(sc[0], k, exact_halves)
    v2, rank2 = _top_list(sc[1], k, exact_halves)
    cand = jnp.concatenate([v1[i] + v2[j] for (i, j) in _CAND], axis=0)
    _, crank = _top_list(cand, k, True)
    sel = crank < k
    ev1 = [jnp.exp(v - v1[0]) for v in v1]
    ev2 = [jnp.exp(v - v2[0]) for v in v2]
    zsum = jnp.zeros_like(v1[0])
    row_len = [jnp.zeros(v1[0].shape, F32) for _ in range(k)]
    for r, (i, j) in enumerate(_CAND):
        s_r = sel[r:r + 1, :]
        zsum = zsum + jnp.where(s_r, ev1[i] * ev2[j], 0.0)
        row_len[i] = row_len[i] + jnp.where(s_r, 1.0, 0.0)
    len_a = jnp.zeros(rank1.shape, F32)
    for i in range(k):
        len_a = jnp.where(rank1 == i, row_len[i], len_a)
    len_ref[:, h, :] = _bf16_pair_word(len_a)
    e1_ref[:, h, :] = _bf16_pair_word(jnp.exp(sc[0] - v1[0]))
    r2_ref[h] = pltpu.bitcast(rank2.astype(BF16), I32)
    e2_ref[h] = pltpu.bitcast((jnp.exp(sc[1] - v2[0]) / zsum).astype(BF16), I32)
    if exact_halves:
        return None
    return _ranked_count_off(rank1, k) | _ranked_count_off(rank2, k)


def _route_kernel(q_ref, keys_ref, *out_refs):
    tie = None
    for h in range(PEER_HEADS):
        t = _route_head(h, False, q_ref, keys_ref, *out_refs)
        tie = t if tie is None else tie | t

    @pl.when(jnp.max(jnp.where(tie, 1, 0)) > 0)
    def _():
        for h in range(PEER_HEADS):
            _route_head(h, True, q_ref, keys_ref, *out_refs)


def _route(q, keys_stack, layer):
    t = q.shape[0]
    tr = TR_ROUTE
    shp1 = (PEER_NKEYS, PEER_HEADS, t)
    spec1 = pl.BlockSpec((PEER_NKEYS, PEER_HEADS, tr), lambda i: (0, 0, i))
    shp2 = (PEER_HEADS, PEER_NKEYS // 2, t)
    spec2 = pl.BlockSpec((PEER_HEADS, PEER_NKEYS // 2, tr), lambda i: (0, 0, i))
    return pl.pallas_call(
        _route_kernel,
        out_shape=(jax.ShapeDtypeStruct(shp1, I32), jax.ShapeDtypeStruct(shp1, I32),
                   jax.ShapeDtypeStruct(shp2, I32), jax.ShapeDtypeStruct(shp2, I32)),
        grid=(t // tr,),
        in_specs=[pl.BlockSpec((tr, q.shape[1]), lambda i: (i, 0)),
                  pl.BlockSpec((None,) + keys_stack.shape[1:], lambda i: (layer, 0, 0, 0, 0))],
        out_specs=(spec1, spec1, spec2, spec2),
        compiler_params=_cparams(("parallel",)), name="peer_route",
    )(q, keys_stack)


def _gelu_gate(x, g):
    c = 2.0 * math.sqrt(2.0 / math.pi) * math.log2(math.e)
    e = jnp.exp2(x * (-c - (c * 0.044715) * (x * x)))
    return (x * g) / (1.0 + e)


def _bf16_rows(word_row):
    words = jnp.broadcast_to(word_row, (PEER_NKEYS // 2, word_row.shape[1]))
    return pltpu.bitcast(words, BF16)


def _peer_dense_kernel(n_steps, h_ref, len_ref, e1_ref, r2_ref, e2_ref, u_ref, v_ref, y_ref, s0, s1, ga):
    step = pl.program_id(0)
    nt = (((1,), (1,)), ((), ()))
    tm = h_ref.shape[0]
    d = v_ref.shape[1]
    half_e = E_STEP // 2

    @pl.when(step == 0)
    def _():
        s0[...] = jnp.zeros_like(s0)
        s1[...] = jnp.zeros_like(s1)

    @pl.when((step == 0) | (lax.rem(step + n_steps - 1, n_steps) == 0))
    def _():
        y_ref[...] = jnp.zeros_like(y_ref)

    def body(s_new, s_old):
        def scores(half):
            rows = slice(half * half_e, (half + 1) * half_e)
            s_new[:, rows] = lax.dot_general(h_ref[...], u_ref[rows, :].astype(BF16), nt,
                                             preferred_element_type=F32)

        def gates(r, c):
            tok = slice(c * LANES, (c + 1) * LANES)
            gt = None
            for hd in range(PEER_HEADS):
                r2 = pltpu.bitcast(r2_ref[hd, :, tok], BF16)
                e2 = pltpu.bitcast(e2_ref[hd, :, tok], BF16)
                hit = r2 < _bf16_rows(len_ref[r, hd:hd + 1, tok])
                term = jnp.where(hit, e2, 0.0) * _bf16_rows(e1_ref[r, hd:hd + 1, tok])
                gt = term if gt is None else gt + term
            cols = slice(r * PEER_NKEYS, (r + 1) * PEER_NKEYS)
            ga[tok, cols] = _gelu_gate(s_old[tok, cols].astype(BF16), gt.T)

        def contract(piece, n_pieces):
            w = d // n_pieces
            cols = slice(piece * w, (piece + 1) * w)
            y_ref[:, cols] += jnp.dot(ga[...], v_ref[:, cols].astype(BF16), preferred_element_type=F32)

        scores(0)
        scores(1)
        for r in range(A_STEP):
            for c in range(tm // LANES):
                gates(r, c)
        for piece in range(4):
            contract(piece, 4)

    even = lax.rem(step, 2) == 0

    @pl.when(even)
    def _():
        body(s0, s1)

    @pl.when(jnp.logical_not(even))
    def _():
        body(s1, s0)


def _peer_dense(h, route, u_stack, v_stack, layer):
    t, d = h.shape
    e = u_stack.shape[1]
    tm = min(TM_PEER, t)
    n_steps = e // E_STEP
    len_a, e1, r2, e2 = route
    n_pairs = (t // tm) * n_steps

    def pair(g, lag):
        p = jnp.clip(g - lag, 0, n_pairs - 1)
        return p // n_steps, p % n_steps

    aspec = pl.BlockSpec((A_STEP, PEER_HEADS, tm), lambda g: (pair(g, 1)[1], 0, pair(g, 1)[0]))
    bspec = pl.BlockSpec((PEER_HEADS, PEER_NKEYS // 2, tm), lambda g: (0, 0, pair(g, 1)[0]))
    return pl.pallas_call(
        functools.partial(_peer_dense_kernel, n_steps),
        out_shape=jax.ShapeDtypeStruct((t, d), F32),
        grid=(n_pairs + 1,),
        in_specs=[pl.BlockSpec((tm, d), lambda g: (pair(g, 0)[0], 0)),
                  aspec, aspec, bspec, bspec,
                  pl.BlockSpec((None, E_STEP, d), lambda g: (layer, pair(g, 0)[1], 0)),
                  pl.BlockSpec((None, E_STEP, d), lambda g: (layer, pair(g, 1)[1], 0))],
        out_specs=pl.BlockSpec((tm, d), lambda g: (pair(g, 1)[0], 0)),
        scratch_shapes=[pltpu.VMEM((tm, E_STEP), F32), pltpu.VMEM((tm, E_STEP), F32),
                        pltpu.VMEM((tm, E_STEP), BF16)],
        compiler_params=_cparams(("arbitrary",)), name="peer_dense",
    )(h, len_a, e1, r2, e2, u_stack, v_stack)


def kernel(x_prompt, x_sample, cache_k, cache_v, c, c_ctx, norm_g, final_g, w_mod, b_mod, attn_w_qkv, attn_q_g, attn_k_g, attn_w_o, sgu_w_in, sgu_g, sgu_w_s, sgu_b, sgu_w_out, conv_w_in, conv_w, conv_w_out, peer_w_q, peer_keys, peer_u, peer_v):
    nb, sl, d = x_prompt.shape
    db, dsl, _ = x_sample.shape
    depth = w_mod.shape[0]
    n_ctx = nb * sl
    assert d == D_MODEL and dsl == SEQ_ALIGN and n_ctx == N_CTX_TOKENS and SEQ_ALIGN % sl == 0
    assert 1 + db <= MOD_ROWS

    x = jnp.concatenate([x_prompt.reshape(n_ctx, d), x_sample.reshape(db * dsl, d)], axis=0)
    cvec = jnp.concatenate([c_ctx[None, :], c, jnp.zeros((MOD_ROWS - 1 - db, d), F32)], axis=0)
    mods = _modulation(cvec, w_mod, b_mod).reshape(depth, MOD_ROWS, N_MOD, 1, d)

    cos_f, sin_s = _rope_tables(dsl)
    nq, nk = N_HEADS * HEAD_DIM, N_KV_HEADS * HEAD_DIM
    new_k, new_v = [], []
    y = None
    for i in range(depth):
        kind, j = i % N_MIXERS, i // N_MIXERS
        x, h = _norm_mod(x, y, mods, (i - 1, 5), (i, 0), (i, 1), norm_g[i, 0])
        if kind == 0:
            qkv = _matmul(h, attn_w_qkv, j)
            g_full = jnp.concatenate([jnp.tile(attn_q_g[j], N_HEADS), jnp.tile(attn_k_g[j], N_KV_HEADS)])
            qn, kn = _qk_post(qkv, g_full.reshape(1, nq + nk), cos_f, sin_s, n_ctx)
            new_k.append(kn[:n_ctx].reshape(nb, sl, N_KV_HEADS, HEAD_DIM))
            new_v.append(qkv[:n_ctx, nq + nk:].reshape(nb, sl, N_KV_HEADS, HEAD_DIM))
            o_ctx = _attention(qn, kn, qkv, 0, nb, sl)
            ck = jnp.transpose(cache_k[:, j], (0, 2, 1, 3))
            cv = jnp.transpose(cache_v[:, j], (0, 2, 1, 3))
            o_lat = _attention(qn, kn, qkv, n_ctx, db, dsl, ck, cv)
            o = _matmul(jnp.concatenate([o_ctx, o_lat], axis=0), attn_w_o, j)
        elif kind == 1:
            z = _matmul(h, sgu_w_in, j, act="gelu")
            b_full = jnp.repeat(sgu_b[j].T, SGU_CHUNK, axis=1)
            o = _matmul(_sgu_mid(z, sgu_g[j], sgu_w_s[j], b_full), sgu_w_out, j)
        else:
            yc = _matmul(h, conv_w_in, j)
            cw = jnp.concatenate([conv_w[j], jnp.zeros((8 - conv_w.shape[1], d), F32)], axis=0)
            o = _matmul(_conv_mid(yc, cw, n_ctx, sl, dsl), conv_w_out, j)
        x, h = _norm_mod(x, o, mods, (i, 2), (i, 3), (i, 4), norm_g[i, 1])
        q = _matmul(h, peer_w_q, i, out_dtype=BF16)
        y = _peer_dense(h, _route(q, peer_keys, i), peer_u, peer_v, i)

    out = _final_norm(x, y, mods, (depth - 1, 5), final_g)
    y_prompt = out[:n_ctx].reshape(nb, sl, d)
    y_sample = out[n_ctx:].reshape(db, dsl, d)
    return (y_prompt, y_sample, jnp.stack(new_k, axis=1), jnp.stack(new_v, axis=1))
```
